```python
import math
import jax, jax.numpy as jnp
from jax import lax
import numpy as np

D_MODEL = 1024
BATCH = 2
SEQ = 8192
DEPTH = 1

MLA_HEADS = 4
MLA_Q_RANK = 256
MLA_KV_RANK = 128
MLA_NOPE = 128
MLA_ROPE = 64
MLA_VDIM = 128
ROPE_THETA = 10000.0
Q_BLOCK = 128
GDN_HEADS = 4
GDN_DK = 128
GDN_DV = 128
CONV_K = 5
CHUNK = 64
MLA_WIDTH = MLA_HEADS * MLA_VDIM
GDN_WIDTH = GDN_HEADS * GDN_DV
MIX_WIDTH = MLA_WIDTH + GDN_WIDTH
MLA_IN = MLA_Q_RANK + MLA_KV_RANK + MLA_ROPE
GDN_QKV = GDN_HEADS * (2 * GDN_DK + GDN_DV)
GDN_IN = GDN_QKV + GDN_WIDTH + 4 * GDN_HEADS
IN_WIDTH = MLA_IN + GDN_IN
PEER_HEADS = 8
PEER_NKEYS = 128
PEER_EXPERTS = PEER_NKEYS * PEER_NKEYS
PEER_TOPK = 16
PEER_QDIM = 256
PEER_HALF = PEER_QDIM // 2
TOKEN_BLOCK = 128
EPS = 1e-6

kernel_name = "hybrid_mla_gdn_peer_encoder"

F32 = jnp.float32


def rmsnorm(t, w):
    tf = t.astype(F32)
    y = tf * lax.rsqrt(jnp.mean(tf * tf, axis=-1, keepdims=True) + EPS)
    return (y * w.astype(F32)).astype(t.dtype)


def l2norm(t):
    tf = t.astype(F32)
    return (tf * lax.rsqrt(jnp.sum(tf * tf, axis=-1, keepdims=True) + EPS)).astype(t.dtype)


def rope_tables(positions):
    inv = ROPE_THETA ** (-jnp.arange(0, MLA_ROPE, 2, dtype=F32) / MLA_ROPE)
    ang = positions.astype(F32)[..., None] * inv
    return jnp.cos(ang)[:, :, None, :], jnp.sin(ang)[:, :, None, :]


def apply_rope(t, cos, sin):
    half = t.shape[-1] // 2
    t1, t2 = t[..., :half], t[..., half:]
    return jnp.concatenate([t1 * cos - t2 * sin, t2 * cos + t1 * sin], axis=-1).astype(t.dtype)


def block_dense_attention(q, k, v):
    B, S, H, D = q.shape
    scale = D ** -0.5
    kf = k.astype(F32)
    vf = v.astype(F32)
    qb = jnp.moveaxis(q.reshape(B, S // Q_BLOCK, Q_BLOCK, H, D), 1, 0)

    def attend(q_blk):
        s = jnp.einsum('bqhd,bkhd->bhqk', q_blk.astype(F32), kf) * scale
        p = jax.nn.softmax(s, axis=-1)
        return jnp.einsum('bhqk,bkhd->bqhd', p, vf)

    o = lax.map(attend, qb)
    return jnp.moveaxis(o, 0, 1).reshape(B, S, H, v.shape[-1]).astype(q.dtype)


def mla_group(mla_in, cos, sin, q_a_norm_w, w_q_up, kv_a_norm_w, w_kv_up):
    B, S, _ = mla_in.shape
    c_q = rmsnorm(mla_in[..., :MLA_Q_RANK], q_a_norm_w)
    c_kv = rmsnorm(mla_in[..., MLA_Q_RANK:MLA_Q_RANK + MLA_KV_RANK], kv_a_norm_w)
    k_pe = mla_in[..., MLA_Q_RANK + MLA_KV_RANK:]
    q = (c_q @ w_q_up).reshape(B, S, MLA_HEADS, MLA_NOPE + MLA_ROPE)
    q = jnp.concatenate([q[..., :MLA_NOPE], apply_rope(q[..., MLA_NOPE:], cos, sin)], axis=-1)
    kv = (c_kv @ w_kv_up).reshape(B, S, MLA_HEADS, MLA_NOPE + MLA_VDIM)
    k_pe = apply_rope(k_pe[:, :, None, :], cos, sin)
    k = jnp.concatenate([kv[..., :MLA_NOPE], jnp.broadcast_to(k_pe, (B, S, MLA_HEADS, MLA_ROPE))], axis=-1)
    v = kv[..., MLA_NOPE:]
    o = block_dense_attention(q, k, v)
    return o.reshape(B, S, MLA_WIDTH)


def centred_depthwise_conv(t, w):
    C = t.shape[-1]
    pad = CONV_K // 2
    return lax.conv_general_dilated(
        t, w.astype(t.dtype)[:, None, :], window_strides=(1,), padding=[(pad, pad)],
        dimension_numbers=('NWC', 'WIO', 'NWC'), feature_group_count=C)


def gated_delta_chunked(q, k, v, g, beta):
    dtype = v.dtype
    q, k, v, g, beta = (t.astype(F32) for t in (q, k, v, g, beta))
    B, H, S, DK = k.shape
    DV = v.shape[-1]
    NC = S // CHUNK
    q = q * (DK ** -0.5)

    def chunks(t):
        return t.reshape(B, H, NC, CHUNK, *t.shape[3:])

    q, k, v, g, beta = (chunks(t) for t in (q, k, v, g, beta))
    g = jnp.cumsum(g, axis=-1)
    idx = jnp.arange(CHUNK)
    incl = idx[:, None] >= idx[None, :]
    strict = idx[:, None] > idx[None, :]
    diff = g[..., :, None] - g[..., None, :]
    decay = jnp.where(incl, jnp.exp(jnp.where(incl, diff, 0.0)), 0.0)
    k_beta = k * beta[..., None]
    v_beta = v * beta[..., None]
    lower = jnp.where(strict, jnp.einsum('bhnid,bhnjd->bhnij', k_beta, k) * decay, 0.0)
    a_mat = lower + jnp.eye(CHUNK, dtype=F32)
    rhs = jnp.concatenate([v_beta, k_beta * jnp.exp(g)[..., None]], axis=-1)
    sol = lax.linalg.triangular_solve(a_mat, rhs, left_side=True, lower=True, unit_diagonal=True)
    u, w = sol[..., :DV], sol[..., DV:]
    attn = jnp.where(incl, jnp.einsum('bhnid,bhnjd->bhnij', q, k) * decay, 0.0)
    g_last = g[..., -1]
    q_dec = q * jnp.exp(g)[..., None]
    k_dec = k * jnp.exp(g_last[..., None] - g)[..., None]
    xs = tuple(jnp.moveaxis(t, 2, 0) for t in (q_dec, k_dec, u, w, attn, g_last))

    def step(state, inp):
        qd, kd, uc, wc, ac, gl = inp
        v_new = uc - jnp.einsum('bhck,bhkv->bhcv', wc, state)
        o = jnp.einsum('bhck,bhkv->bhcv', qd, state) + jnp.einsum('bhij,bhjv->bhiv', ac, v_new)
        state = state * jnp.exp(gl)[..., None, None] + jnp.einsum('bhck,bhcv->bhkv', kd, v_new)
        return state, o

    state0 = jnp.zeros((B, H, DK, DV), F32)
    _, o = lax.scan(step, state0, xs)
    o = jnp.moveaxis(o, 0, 2).reshape(B, H, S, DV)
    return o.astype(dtype)


def gdn_group(gdn_in, conv_w, a_log_f, dt_bias_f, a_log_b, dt_bias_b, norm_w):
    B, S, _ = gdn_in.shape
    qkv = gdn_in[..., :GDN_QKV]
    z = gdn_in[..., GDN_QKV:GDN_QKV + GDN_WIDTH].reshape(B, S, GDN_HEADS, GDN_DV)
    gates = gdn_in[..., GDN_QKV + GDN_WIDTH:].reshape(B, S, 4, GDN_HEADS)
    a_f, b_f, a_b, b_b = gates[:, :, 0], gates[:, :, 1], gates[:, :, 2], gates[:, :, 3]
    qkv = jax.nn.silu(centred_depthwise_conv(qkv, conv_w))
    nq = GDN_HEADS * GDN_DK
    q = l2norm(qkv[..., :nq].reshape(B, S, GDN_HEADS, GDN_DK))
    k = l2norm(qkv[..., nq:2 * nq].reshape(B, S, GDN_HEADS, GDN_DK))
    v = qkv[..., 2 * nq:].reshape(B, S, GDN_HEADS, GDN_DV)
    af = a_f.astype(F32)
    ab = a_b.astype(F32)
    g_f = -jnp.exp(a_log_f.astype(F32)) * jax.nn.softplus(af + dt_bias_f.astype(F32))
    g_b = -jnp.exp(a_log_b.astype(F32)) * jax.nn.softplus(ab + dt_bias_b.astype(F32))
    beta_f = jax.nn.sigmoid(b_f.astype(F32))
    beta_b = jax.nn.sigmoid(b_b.astype(F32))
    qt, kt, vt = (jnp.moveaxis(t, 2, 1) for t in (q, k, v))
    g_ft, g_bt, be_ft, be_bt = (jnp.moveaxis(t, 2, 1) for t in (g_f, g_b, beta_f, beta_b))
    o_f = gated_delta_chunked(qt, kt, vt, g_ft, be_ft)
    flip = lambda t: jnp.flip(t, axis=2)
    o_b = flip(gated_delta_chunked(flip(qt), flip(kt), flip(vt), flip(g_bt), flip(be_bt)))
    o = jnp.moveaxis(o_f + o_b, 1, 2)
    o = rmsnorm(o, norm_w) * jax.nn.silu(z)
    return o.reshape(B, S, GDN_WIDTH)


def peer_ffn(xn, w_q, sub_keys, u_tab, v_tab):
    B, S, D = xn.shape
    T = B * S
    xt = xn.reshape(T, D)
    q = (xt @ w_q).reshape(T, PEER_HEADS, 2, PEER_HALF)
    scores = jnp.einsum('thpd,hpnd->thpn', q, sub_keys).astype(F32)
    s_top, i_top = lax.top_k(scores, PEER_TOPK)
    cand = s_top[:, :, 0, :, None] + s_top[:, :, 1, None, :]
    cand_idx = i_top[:, :, 0, :, None] * PEER_NKEYS + i_top[:, :, 1, None, :]
    f_s, f_i = lax.top_k(cand.reshape(T, PEER_HEADS, PEER_TOPK * PEER_TOPK), PEER_TOPK)
    experts = jnp.take_along_axis(cand_idx.reshape(T, PEER_HEADS, PEER_TOPK * PEER_TOPK), f_i, axis=-1)
    gates = jax.nn.softmax(f_s, axis=-1)
    nb = T // TOKEN_BLOCK
    xb = xt.reshape(nb, TOKEN_BLOCK, D)
    eb = experts.reshape(nb, TOKEN_BLOCK, PEER_HEADS * PEER_TOPK)
    gb = gates.reshape(nb, TOKEN_BLOCK, PEER_HEADS * PEER_TOPK)

    def blk(args):
        xc, ec, gc = args
        uc = u_tab[ec]
        hid = jax.nn.gelu(jnp.einsum('td,ted->te', xc, uc).astype(F32), approximate=False)
        vc = v_tab[ec]
        return jnp.einsum('te,ted->td', (gc * hid).astype(vc.dtype), vc)

    out = lax.map(blk, (xb, eb, gb))
    return out.reshape(B, S, D).astype(xn.dtype)


def setup_inputs(seed: int = 0) -> dict:
    key = jax.random.key(seed)
    ks = jax.random.split(key, 24)
    L = DEPTH

    def nrm(k, shape, scale):
        return jax.random.normal(k, shape, F32) * scale

    def gain(k, shape):
        return 1.0 + 0.01 * jax.random.normal(k, shape, F32)

    def dt_bias(k):
        dt = jnp.exp(jax.random.uniform(k, (L, GDN_HEADS), F32, math.log(1e-3), math.log(1e-1)))
        return dt + jnp.log(-jnp.expm1(-dt))

    x = nrm(ks[0], (BATCH, SEQ, D_MODEL), 1.0)
    positions = jnp.arange(SEQ, dtype=jnp.int32)[None, :] + jax.random.randint(ks[1], (BATCH, 1), 0, 1024, dtype=jnp.int32)
    return {
        'x': x,
        'positions': positions,
        'ln1_w': gain(ks[2], (L, D_MODEL)),
        'w_in': nrm(ks[3], (L, D_MODEL, IN_WIDTH), D_MODEL ** -0.5),
        'q_a_norm_w': gain(ks[4], (L, MLA_Q_RANK)),
        'w_q_up': nrm(ks[5], (L, MLA_Q_RANK, MLA_HEADS * (MLA_NOPE + MLA_ROPE)), MLA_Q_RANK ** -0.5),
        'kv_a_norm_w': gain(ks[6], (L, MLA_KV_RANK)),
        'w_kv_up': nrm(ks[7], (L, MLA_KV_RANK, MLA_HEADS * (MLA_NOPE + MLA_VDIM)), MLA_KV_RANK ** -0.5),
        'conv_w': nrm(ks[8], (L, CONV_K, GDN_QKV), CONV_K ** -0.5),
        'a_log_f': jnp.log(jax.random.uniform(ks[9], (L, GDN_HEADS), F32, 1.0, 16.0)),
        'dt_bias_f': dt_bias(ks[10]),
        'a_log_b': jnp.log(jax.random.uniform(ks[11], (L, GDN_HEADS), F32, 1.0, 16.0)),
        'dt_bias_b': dt_bias(ks[12]),
        'gdn_norm_w': gain(ks[13], (L, GDN_DV)),
        'w_out': nrm(ks[14], (L, MIX_WIDTH, D_MODEL), MIX_WIDTH ** -0.5),
        'ln2_w': gain(ks[15], (L, D_MODEL)),
        'peer_wq': nrm(ks[16], (L, D_MODEL, PEER_HEADS * PEER_QDIM), D_MODEL ** -0.5),
        'peer_sub_keys': nrm(ks[17], (L, PEER_HEADS, 2, PEER_NKEYS, PEER_HALF), PEER_HALF ** -0.5),
        'peer_u': nrm(ks[18], (L, PEER_EXPERTS, D_MODEL), D_MODEL ** -0.5),
        'peer_v': nrm(ks[19], (L, PEER_EXPERTS, D_MODEL), PEER_HEADS ** -0.5),
        'final_norm_w': gain(ks[20], (D_MODEL,)),
    }


def reference(x, positions, ln1_w, w_in, q_a_norm_w, w_q_up, kv_a_norm_w, w_kv_up, conv_w,
              a_log_f, dt_bias_f, a_log_b, dt_bias_b, gdn_norm_w, w_out, ln2_w,
              peer_wq, peer_sub_keys, peer_u, peer_v, final_norm_w):
    cos, sin = rope_tables(positions)
    h = x
    for i in range(DEPTH):
        n1 = rmsnorm(h, ln1_w[i])
        proj = n1 @ w_in[i]
        o_mla = mla_group(proj[..., :MLA_IN], cos, sin, q_a_norm_w[i], w_q_up[i], kv_a_norm_w[i], w_kv_up[i])
        o_gdn = gdn_group(proj[..., MLA_IN:], conv_w[i], a_log_f[i], dt_bias_f[i], a_log_b[i], dt_bias_b[i], gdn_norm_w[i])
        h = h + jnp.concatenate([o_mla, o_gdn], axis=-1) @ w_out[i]
        n2 = rmsnorm(h, ln2_w[i])
        h = h + peer_ffn(n2, peer_wq[i], peer_sub_keys[i], peer_u[i], peer_v[i])
    return rmsnorm(h, final_norm_w)
```

```python
import functools

import jax
import jax.numpy as jnp
from jax import lax
from jax.experimental import pallas as pl
from jax.experimental.pallas import tpu as pltpu

F32 = jnp.float32
BF16 = jnp.bfloat16

D_MODEL = 1024
BATCH = 2
SEQ = 8192
TOKENS = BATCH * SEQ

MLA_HEADS = 4
MLA_Q_RANK = 256
MLA_KV_RANK = 128
MLA_NOPE = 128
MLA_ROPE = 64
MLA_VDIM = 128
ROPE_THETA = 10000.0
MLA_QK = MLA_NOPE + MLA_ROPE
MLA_QPAD = 256

GDN_HEADS = 4
GDN_DK = 128
GDN_DV = 128
CONV_K = 5
CHUNK = 64
GDN_W = GDN_HEADS * GDN_DK
GDN_QKV = 3 * GDN_W
N_CHUNKS = SEQ // CHUNK

PEER_HEADS = 8
PEER_NKEYS = 128
PEER_EXPERTS = PEER_NKEYS * PEER_NKEYS
PEER_TOPK = 16
PEER_QDIM = 256
PEER_HALF = 128
PEER_SEL = PEER_HEADS * PEER_TOPK
EPS = 1e-6

LANES = 128
VMEM_LIMIT = 56 * 1024 * 1024

TM_INPROJ = 1024
TM_MLA = 512
TQ_ATTN = 512
TK_ATTN = 2048
TM_GDN = 512
R_CHUNK = 256
U_ROWS = 128
SCAN_CH = 8
TM_MIX = 512
TT_TOPK = 128
TOPK_HEADS_PER_ITER = 2
TOPK_PAIRS = PEER_HEADS // TOPK_HEADS_PER_ITER
TT_PEER = 512
TE_PEER = 1024
PEER_SUB = 2
A_PER = TE_PEER // PEER_NKEYS
A_HALF = PEER_NKEYS // 2
G_PITCH = A_HALF + 8
TOK_UNROLL = 64

NEG_BIG = -1e30


def _const_row(value):
    return jnp.full((1, LANES), value, F32)


def _cparams(sem):
    return pltpu.CompilerParams(dimension_semantics=sem, vmem_limit_bytes=VMEM_LIMIT)


def _dot(a, b):
    return jnp.dot(a, b, preferred_element_type=F32)


def _dot_nt(a, b):
    return lax.dot_general(a, b, (((1,), (1,)), ((), ())), preferred_element_type=F32)


def _inproj_kernel(x_ref, lnw_ref, w_ref, mla_ref, qkv_ref, z_ref, g_ref):
    x = x_ref[...]
    y = x * lax.rsqrt(jnp.mean(x * x, axis=-1, keepdims=True) + EPS) * lnw_ref[...]
    p = _dot(y.astype(BF16), w_ref[...])
    mla_ref[...] = p[:, :512]
    qkv_ref[...] = p[:, 512:512 + GDN_QKV]
    z_ref[...] = p[:, 2048:2560]
    g_ref[...] = p[:, 2560:2688]


def _inproj(x2, ln1_w, w_all):
    tm = TM_INPROJ
    n = w_all.shape[1]
    return pl.pallas_call(
        _inproj_kernel,
        grid=(TOKENS // tm,),
        in_specs=[
            pl.BlockSpec((tm, D_MODEL), lambda i: (i, 0)),
            pl.BlockSpec((1, D_MODEL), lambda i: (0, 0)),
            pl.BlockSpec((D_MODEL, n), lambda i: (0, 0)),
        ],
        out_specs=[
            pl.BlockSpec((tm, 512), lambda i: (i, 0)),
            pl.BlockSpec((tm, GDN_QKV), lambda i: (i, 0)),
            pl.BlockSpec((tm, GDN_W), lambda i: (i, 0)),
            pl.BlockSpec((tm, LANES), lambda i: (i, 0)),
        ],
        out_shape=[
            jax.ShapeDtypeStruct((TOKENS, 512), F32),
            jax.ShapeDtypeStruct((TOKENS, GDN_QKV), F32),
            jax.ShapeDtypeStruct((TOKENS, GDN_W), F32),
            jax.ShapeDtypeStruct((TOKENS, LANES), F32),
        ],
        compiler_params=_cparams(("parallel",)),
        name="inproj",
    )(x2, ln1_w, w_all)


def _mla_prep_kernel(mla_ref, pos_ref, inv_ref, sgn_ref, qnw_ref, kvnw_ref,
                     wqn_ref, wqr_ref, wqrs_ref, wukt_ref, q_ref, k_ref, v_ref):
    m = mla_ref[...]
    cq = m[:, :MLA_Q_RANK]
    ckv = m[:, MLA_Q_RANK:MLA_Q_RANK + MLA_KV_RANK]
    kp = m[:, 384:512]
    cqn = (cq * lax.rsqrt(jnp.mean(cq * cq, axis=-1, keepdims=True) + EPS) * qnw_ref[...]).astype(BF16)
    ckvn = ckv * lax.rsqrt(jnp.mean(ckv * ckv, axis=-1, keepdims=True) + EPS) * kvnw_ref[...]
    ang = pos_ref[...] * inv_ref[...]
    cos = jnp.cos(ang)
    sin = jnp.sin(ang) * sgn_ref[...]
    cos4 = jnp.concatenate([cos] * MLA_HEADS, axis=1)
    sin4 = jnp.concatenate([sin] * MLA_HEADS, axis=1)
    qn = _dot(cqn, wqn_ref[...])
    qr = _dot(cqn, wqr_ref[...]) * cos4 + _dot(cqn, wqrs_ref[...]) * sin4
    scale = lax.rsqrt(_const_row(float(MLA_QK))) / jnp.log(_const_row(2.0))
    scale = jnp.concatenate([scale, scale], axis=1)
    for h in range(MLA_HEADS):
        qa = _dot(qn[:, h * 128:(h + 1) * 128].astype(BF16), wukt_ref[h])
        q_ref[h] = (jnp.concatenate([qa, qr[:, h * 128:(h + 1) * 128]], axis=1) * scale).astype(BF16)
    lane = lax.broadcasted_iota(jnp.int32, kp.shape, 1)
    kr = kp * cos + pltpu.roll(kp, 64, axis=1) * sin
    kr = jnp.where(lane < MLA_ROPE, kr, 0.0)
    k_ref[...] = jnp.concatenate([ckvn, kr], axis=1).astype(BF16)
    ones_col = jnp.where(lane == 0, 1.0, 0.0)
    v_ref[...] = jnp.concatenate([ckvn, ones_col], axis=1).astype(BF16)


def _mla_prep(mla, pos, inv128, sgn128, qnw, kvnw, wqn, wqr, wqrs, wukt):
    tm = TM_MLA
    full = lambda shape: pl.BlockSpec(shape, lambda i: (0,) * len(shape))
    return pl.pallas_call(
        _mla_prep_kernel,
        grid=(TOKENS // tm,),
        in_specs=[
            pl.BlockSpec((tm, 512), lambda i: (i, 0)),
            pl.BlockSpec((tm, 1), lambda i: (i, 0)),
            full((1, LANES)), full((1, LANES)), full((1, MLA_Q_RANK)), full((1, MLA_KV_RANK)),
            full((MLA_Q_RANK, 512)), full((MLA_Q_RANK, 512)), full((MLA_Q_RANK, 512)),
            full((MLA_HEADS, 128, 128)),
        ],
        out_specs=[
            pl.BlockSpec((MLA_HEADS, tm, MLA_QPAD), lambda i: (0, i, 0)),
            pl.BlockSpec((tm, MLA_QPAD), lambda i: (i, 0)),
            pl.BlockSpec((tm, MLA_QPAD), lambda i: (i, 0)),
        ],
        out_shape=[
            jax.ShapeDtypeStruct((MLA_HEADS, TOKENS, MLA_QPAD), BF16),
            jax.ShapeDtypeStruct((TOKENS, MLA_QPAD), BF16),
            jax.ShapeDtypeStruct((TOKENS, MLA_QPAD), BF16),
        ],
        compiler_params=_cparams(("parallel",)),
        name="mla_prep",
    )(mla, pos, inv128, sgn128, qnw, kvnw, wqn, wqr, wqrs, wukt)


def _attn_kernel(q_ref, k_ref, v_ref, wuv_ref, o_ref, m_ref, acc_ref):
    j = pl.program_id(2)
    tq = TQ_ATTN

    @pl.when(j == 0)
    def _():
        m_ref[...] = jnp.full(m_ref.shape, NEG_BIG, F32)
        acc_ref[...] = jnp.zeros(acc_ref.shape, F32)

    k = k_ref[...]
    v = v_ref[...]
    hsl = [slice(h * tq, (h + 1) * tq) for h in range(MLA_HEADS)]
    ss = [_dot_nt(q_ref[h], k) for h in range(MLA_HEADS)]
    m_prevs = [m_ref[hs, :] for hs in hsl]
    m_news = [jnp.maximum(mp, jnp.max(s, axis=1, keepdims=True)) for mp, s in zip(m_prevs, ss)]
    alphas = [jnp.exp2(mp - mn) for mp, mn in zip(m_prevs, m_news)]
    ps = [jnp.exp2((s - jnp.concatenate([mn] * (TK_ATTN // LANES), axis=1)).astype(BF16)) for s, mn in zip(ss, m_news)]
    pvs = [_dot(p, v) for p in ps]
    for hs, al, pv, mn in zip(hsl, alphas, pvs, m_news):
        acc_ref[hs, :] = jnp.concatenate([al] * (MLA_QPAD // LANES), axis=1) * acc_ref[hs, :] + pv
        m_ref[hs, :] = mn

    @pl.when(j == pl.num_programs(2) - 1)
    def _():
        outs = []
        for h, hs in enumerate(hsl):
            o_lat = acc_ref[hs, :MLA_KV_RANK] / acc_ref[hs, MLA_KV_RANK:MLA_KV_RANK + 1]
            outs.append(_dot(o_lat.astype(BF16), wuv_ref[h]))
        o_ref[...] = jnp.concatenate(outs, axis=1)


def _attention(q, k, v, wuv):
    tq, tk = TQ_ATTN, TK_ATTN
    nq, nk = SEQ // tq, SEQ // tk
    return pl.pallas_call(
        _attn_kernel,
        grid=(BATCH, nq, nk),
        in_specs=[
            pl.BlockSpec((MLA_HEADS, tq, MLA_QPAD), lambda b, i, j: (0, b * nq + i, 0)),
            pl.BlockSpec((tk, MLA_QPAD), lambda b, i, j: (b * nk + j, 0)),
            pl.BlockSpec((tk, MLA_QPAD), lambda b, i, j: (b * nk + j, 0)),
            pl.BlockSpec((MLA_HEADS, 128, 128), lambda b, i, j: (0, 0, 0)),
        ],
        out_specs=pl.BlockSpec((tq, MLA_HEADS * MLA_VDIM), lambda b, i, j: (b * nq + i, 0)),
        out_shape=jax.ShapeDtypeStruct((TOKENS, MLA_HEADS * MLA_VDIM), F32),
        scratch_shapes=[
            pltpu.VMEM((MLA_HEADS * tq, LANES), F32),
            pltpu.VMEM((MLA_HEADS * tq, MLA_QPAD), F32),
        ],
        compiler_params=_cparams(("parallel", "parallel", "arbitrary")),
        name="attn",
    )(q, k, v, wuv)


def _softplus(x):
    return jnp.maximum(x, 0.0) + jnp.log1p(jnp.exp(-jnp.abs(x)))


def _gdn_prep_kernel(x_ref, prev_ref, next_ref, g_ref, cw_ref, nalog_ref, dtb_ref,
                     q_ref, k_ref, v_ref, gc_ref, gct_ref):
    i = pl.program_id(0)
    tm = TM_GDN
    tiles_per_seq = SEQ // tm
    first = (i % tiles_per_seq) == 0
    last = (i % tiles_per_seq) == tiles_per_seq - 1
    prev = jnp.where(first, 0.0, prev_ref[...])
    nxt = jnp.where(last, 0.0, next_ref[...])
    xe = jnp.concatenate([prev, x_ref[...], nxt], axis=0)
    pad = CONV_K // 2
    y = None
    for t in range(CONV_K):
        off = 8 - pad + t
        term = xe[off:off + tm, :] * cw_ref[t:t + 1, :]
        y = term if y is None else y + term
    y = y * jax.nn.sigmoid(y)

    def l2n(a):
        return a * lax.rsqrt(jnp.sum(a * a, axis=-1, keepdims=True) + EPS)

    qs = lax.rsqrt(_const_row(float(GDN_DK)))
    for h in range(GDN_HEADS):
        hs = slice(h * GDN_DK, (h + 1) * GDN_DK)
        q_ref[:, hs] = l2n(y[:, h * GDN_DK:(h + 1) * GDN_DK]) * qs
        k_ref[:, hs] = l2n(y[:, GDN_W + h * GDN_DK:GDN_W + (h + 1) * GDN_DK])
    v_ref[...] = y[:, 2 * GDN_W:]

    g = g_ref[...]
    gdec = nalog_ref[...] * _softplus(g + dtb_ref[...])
    beta = jax.nn.sigmoid(g)
    row = lax.broadcasted_iota(jnp.int32, g.shape, 0)
    lane = lax.broadcasted_iota(jnp.int32, g.shape, 1)
    pos = row & (CHUNK - 1)
    pre = gdec
    suf = gdec
    s = 1
    while s < CHUNK:
        pre = pre + jnp.where(pos >= s, pltpu.roll(pre, s, axis=0), 0.0)
        suf = suf + jnp.where(pos < CHUNK - s, pltpu.roll(suf, tm - s, axis=0), 0.0)
        s *= 2
    out = jnp.where(lane < 4, pre, jnp.where((lane >= 8) & (lane < 12), suf, beta))
    gc_ref[...] = out
    gct_ref[...] = out.T[:16, :]


def _gdn_prep(qkv, gates, conv_w, nalog, dtb):
    tm = TM_GDN
    nb8 = TOKENS // 8
    return pl.pallas_call(
        _gdn_prep_kernel,
        grid=(TOKENS // tm,),
        in_specs=[
            pl.BlockSpec((tm, GDN_QKV), lambda i: (i, 0)),
            pl.BlockSpec((8, GDN_QKV), lambda i: (jnp.maximum(i * (tm // 8) - 1, 0), 0)),
            pl.BlockSpec((8, GDN_QKV), lambda i: (jnp.minimum((i + 1) * (tm // 8), nb8 - 1), 0)),
            pl.BlockSpec((tm, LANES), lambda i: (i, 0)),
            pl.BlockSpec((8, GDN_QKV), lambda i: (0, 0)),
            pl.BlockSpec((1, LANES), lambda i: (0, 0)),
            pl.BlockSpec((1, LANES), lambda i: (0, 0)),
        ],
        out_specs=[
            pl.BlockSpec((tm, GDN_W), lambda i: (i, 0)),
            pl.BlockSpec((tm, GDN_W), lambda i: (i, 0)),
            pl.BlockSpec((tm, GDN_W), lambda i: (i, 0)),
            pl.BlockSpec((tm, LANES), lambda i: (i, 0)),
            pl.BlockSpec((16, tm), lambda i: (0, i)),
        ],
        out_shape=[
            jax.ShapeDtypeStruct((TOKENS, GDN_W), F32),
            jax.ShapeDtypeStruct((TOKENS, GDN_W), F32),
            jax.ShapeDtypeStruct((TOKENS, GDN_W), F32),
            jax.ShapeDtypeStruct((TOKENS, LANES), F32),
            jax.ShapeDtypeStruct((16, TOKENS), F32),
        ],
        compiler_params=_cparams(("parallel",)),
        name="gdn_prep",
    )(qkv, qkv, qkv, gates, conv_w, nalog, dtb)


def _gdn_chunk_kernel(q_ref, k_ref, v_ref, gc_ref, gct_ref,
                      oi_f, qe_f, kc_f, nc_f, eg_f, oi_b, qe_b, kc_b, nc_b, eg_b):
    U = U_ROWS
    ncu = U // CHUNK
    rowi = lax.broadcasted_iota(jnp.int32, (U, U), 0)
    colj = lax.broadcasted_iota(jnp.int32, (U, U), 1)
    same64 = (rowi >> 6) == (colj >> 6)
    same32 = (rowi >> 5) == (colj >> 5)
    same16 = (rowi >> 4) == (colj >> 4)
    eye = (rowi == colj).astype(F32)
    tri = (rowi >= colj, rowi <= colj)
    stri = (rowi > colj, rowi < colj)
    rchunk = lax.broadcasted_iota(jnp.int32, (U, GDN_DK), 0) >> 6
    gates = gc_ref[...]
    gates_t = gct_ref[...]
    outs = ((oi_f, qe_f, kc_f, nc_f, eg_f), (oi_b, qe_b, kc_b, nc_b, eg_b))

    def bdot(a, b):
        return _dot(a.astype(BF16), b.astype(BF16))

    chains = []
    for u in range(R_CHUNK // U):
        rs = slice(u * U, (u + 1) * U)
        for h in range(GDN_HEADS):
            hs = slice(h * GDN_DK, (h + 1) * GDN_DK)
            q = q_ref[rs, hs]
            k = k_ref[rs, hs]
            v = v_ref[rs, hs]
            kb16 = k.astype(BF16)
            kk = _dot_nt(kb16, kb16)
            qk = _dot_nt(q.astype(BF16), kb16)
            for d in range(2):
                c0 = 8 * d
                gcc = gates[rs, c0 + h:c0 + h + 1]
                bcol = gates[rs, c0 + 4 + h:c0 + 5 + h]
                gcr = gates_t[c0 + h:c0 + h + 1, rs]
                incl = same64 & tri[d]
                dec = jnp.where(incl, jnp.exp(jnp.where(incl, gcc - gcr, 0.0)), 0.0)
                lmat = jnp.where(stri[d], bcol * kk * dec, 0.0)
                chains.append(dict(u=u, h=h, d=d, rs=rs, hs=hs, q=q, k=k, v=v, gcc=gcc, bcol=bcol,
                                   lmat=lmat, attn=jnp.where(incl, qk * dec, 0.0)))

    l0s = [jnp.where(same16, c["lmat"], 0.0) for c in chains]
    xs = [eye - l0 for l0 in l0s]
    ps = [bdot(l0, l0) for l0 in l0s]
    xs = [x + bdot(x, p) for x, p in zip(xs, ps)]
    ps = [bdot(p, p) for p in ps]
    xs = [x + bdot(x, p) for x, p in zip(xs, ps)]
    ps = [bdot(p, p) for p in ps]
    xs = [x + bdot(x, p) for x, p in zip(xs, ps)]
    off32 = same32 & jnp.logical_not(same16)
    ts = [bdot(jnp.where(off32, c["lmat"], 0.0), x) for c, x in zip(chains, xs)]
    xs = [x - bdot(x, t) for x, t in zip(xs, ts)]
    off64 = jnp.logical_not(same32)
    ts = [bdot(jnp.where(off64, c["lmat"], 0.0), x) for c, x in zip(chains, xs)]
    xs = [x - bdot(x, t) for x, t in zip(xs, ts)]

    egs = [jnp.exp(c["gcc"]) for c in chains]
    rhs = [jnp.concatenate([c["v"] * c["bcol"], c["k"] * (c["bcol"] * eg)], axis=1) for c, eg in zip(chains, egs)]
    sols = [bdot(x, r) for x, r in zip(xs, rhs)]
    ows = [bdot(c["attn"], s) for c, s in zip(chains, sols)]
    cats, kdts = [], []
    for c, eg, sol, ow in zip(chains, egs, sols, ows):
        oi_ref, qe_ref = outs[c["d"]][:2]
        oi_ref[c["rs"], c["hs"]] = ow[:, :GDN_DV]
        qe_ref[c["rs"], c["hs"]] = (c["q"] * eg - ow[:, GDN_DV:]).astype(BF16)
        ends = [ch * CHUNK + (CHUNK - 1 if c["d"] == 0 else 0) for ch in range(ncu)]
        glc = jnp.concatenate([jnp.broadcast_to(c["gcc"][r:r + 1, :], (CHUNK, 1)) for r in ends], axis=0)
        kdts.append((c["k"] * jnp.exp(glc - c["gcc"])).T)
        u_part, w_part = sol[:, :GDN_DV], sol[:, GDN_DV:]
        cats.append(jnp.concatenate([jnp.where(rchunk == ch, w_part, 0.0) for ch in range(ncu)]
                                    + [jnp.where(rchunk == ch, u_part, 0.0) for ch in range(ncu)], axis=1))
        c["ends"] = ends
    kns = [bdot(kdt, cat) for kdt, cat in zip(kdts, cats)]
    for c, kn in zip(chains, kns):
        kc_ref, nc_ref, eg_ref = outs[c["d"]][2:]
        for ch in range(ncu):
            ci = c["u"] * ncu + ch
            h = c["h"]
            kc_ref[ci, h] = kn[:, ch * 128:(ch + 1) * 128].astype(BF16)
            nc_ref[ci, h] = kn[:, (ncu + ch) * 128:(ncu + ch + 1) * 128]
            r = c["ends"][ch]
            eg_ref[ci, h:h + 1, :] = jnp.broadcast_to(jnp.exp(c["gcc"][r:r + 1, :]), (1, LANES))


def _gdn_chunk(q, k, v, gc, gct):
    R = R_CHUNK
    ncs = R // CHUNK
    nct = TOKENS // CHUNK
    row_spec = pl.BlockSpec((R, GDN_W), lambda i: (i, 0))
    out_specs, out_shape = [], []
    for _ in range(2):
        out_specs += [
            pl.BlockSpec((R, GDN_W), lambda i: (i, 0)),
            pl.BlockSpec((R, GDN_W), lambda i: (i, 0)),
            pl.BlockSpec((ncs, GDN_HEADS, 128, 128), lambda i: (i, 0, 0, 0)),
            pl.BlockSpec((ncs, GDN_HEADS, 128, 128), lambda i: (i, 0, 0, 0)),
            pl.BlockSpec((ncs, GDN_HEADS, LANES), lambda i: (i, 0, 0)),
        ]
        out_shape += [
            jax.ShapeDtypeStruct((TOKENS, GDN_W), F32),
            jax.ShapeDtypeStruct((TOKENS, GDN_W), BF16),
            jax.ShapeDtypeStruct((nct, GDN_HEADS, 128, 128), BF16),
            jax.ShapeDtypeStruct((nct, GDN_HEADS, 128, 128), F32),
            jax.ShapeDtypeStruct((nct, GDN_HEADS, LANES), F32),
        ]
    return pl.pallas_call(
        _gdn_chunk_kernel,
        grid=(TOKENS // R,),
        in_specs=[row_spec, row_spec, row_spec,
                  pl.BlockSpec((R, LANES), lambda i: (i, 0)),
                  pl.BlockSpec((16, R), lambda i: (0, i))],
        out_specs=out_specs,
        out_shape=out_shape,
        compiler_params=_cparams(("parallel",)),
        name="gdn_chunk",
    )(q, k, v, gc, gct)


def _gdn_scan_kernel(oi_f, qe_f, kc_f, nc_f, eg_f, oi_b, qe_b, kc_b, nc_b, eg_b,
                     of_ref, ob_ref, s_ref):
    c = pl.program_id(1)

    @pl.when(c == 0)
    def _():
        s_ref[...] = jnp.zeros(s_ref.shape, F32)

    dirs = ((oi_f, qe_f, kc_f, nc_f, eg_f, of_ref), (oi_b, qe_b, kc_b, nc_b, eg_b, ob_ref))
    chains = [(d, h) for d in range(2) for h in range(GDN_HEADS)]
    states = [s_ref[d * GDN_HEADS + h] for d, h in chains]
    for t in range(SCAN_CH):
        nxt = []
        for (d, h), s in zip(chains, states):
            oi, qe, kc, nc, eg, o_ref = dirs[d]
            lc = t if d == 0 else SCAN_CH - 1 - t
            rows = slice(lc * CHUNK, (lc + 1) * CHUNK)
            hs = slice(h * GDN_DK, (h + 1) * GDN_DK)
            sb = s.astype(BF16)
            o_ref[rows, hs] = oi[rows, hs] + _dot(qe[rows, hs], sb)
            nxt.append(eg[lc, h:h + 1, :] * s - _dot(kc[lc, h], sb) + nc[lc, h])
        states = nxt
    for (d, h), s in zip(chains, states):
        s_ref[d * GDN_HEADS + h] = s


def _gdn_scan(fwd, bwd):
    nc = N_CHUNKS // SCAN_CH
    rows = SCAN_CH * CHUNK

    def specs(idx):
        return [
            pl.BlockSpec((rows, GDN_W), lambda b, c: (idx(b, c), 0)),
            pl.BlockSpec((rows, GDN_W), lambda b, c: (idx(b, c), 0)),
            pl.BlockSpec((SCAN_CH, GDN_HEADS, 128, 128), lambda b, c: (idx(b, c), 0, 0, 0)),
            pl.BlockSpec((SCAN_CH, GDN_HEADS, 128, 128), lambda b, c: (idx(b, c), 0, 0, 0)),
            pl.BlockSpec((SCAN_CH, GDN_HEADS, LANES), lambda b, c: (idx(b, c), 0, 0)),
        ]

    fidx = lambda b, c: b * nc + c
    bidx = lambda b, c: b * nc + (nc - 1 - c)
    return pl.pallas_call(
        _gdn_scan_kernel,
        grid=(BATCH, nc),
        in_specs=specs(fidx) + specs(bidx),
        out_specs=[
            pl.BlockSpec((rows, GDN_W), lambda b, c: (fidx(b, c), 0)),
            pl.BlockSpec((rows, GDN_W), lambda b, c: (bidx(b, c), 0)),
        ],
        out_shape=[jax.ShapeDtypeStruct((TOKENS, GDN_W), F32)] * 2,
        scratch_shapes=[pltpu.VMEM((2 * GDN_HEADS, GDN_DK, GDN_DV), F32)],
        compiler_params=_cparams(("parallel", "arbitrary")),
        name="gdn_scan",
    )(*fwd, *bwd)


def _mix_kernel(x_ref, om_ref, of_ref, ob_ref, z_ref, gnw_ref, wo_ref, ln2_ref, wq_ref, sk_ref,
                h_ref, n2_ref, st_ref):
    o = of_ref[...] + ob_ref[...]
    z = z_ref[...]
    gparts = []
    for hh in range(GDN_HEADS):
        oh = o[:, hh * GDN_DV:(hh + 1) * GDN_DV]
        oh = oh * lax.rsqrt(jnp.mean(oh * oh, axis=-1, keepdims=True) + EPS) * gnw_ref[...]
        gparts.append(oh)
    og = jnp.concatenate(gparts, axis=1) * (z * jax.nn.sigmoid(z))
    mixed = _dot(om_ref[...].astype(BF16), wo_ref[:512, :]) + _dot(og.astype(BF16), wo_ref[512:, :])
    hres = x_ref[...] + mixed
    h_ref[...] = hres
    n2 = hres * lax.rsqrt(jnp.mean(hres * hres, axis=-1, keepdims=True) + EPS) * ln2_ref[...]
    n2b = n2.astype(BF16)
    n2_ref[...] = n2b
    pq = _dot(n2b, wq_ref[...]).astype(BF16)
    for hp in range(2 * PEER_HEADS):
        st_ref[hp] = _dot_nt(sk_ref[hp], pq[:, hp * PEER_HALF:(hp + 1) * PEER_HALF])


def _mix(x2, o_mla, o_f, o_b, z, gnw, w_out, ln2_w, peer_wq, sub_keys):
    tm = TM_MIX
    nhp = 2 * PEER_HEADS
    full = lambda shape: pl.BlockSpec(shape, lambda i: (0,) * len(shape))
    return pl.pallas_call(
        _mix_kernel,
        grid=(TOKENS // tm,),
        in_specs=[
            pl.BlockSpec((tm, D_MODEL), lambda i: (i, 0)),
            pl.BlockSpec((tm, 512), lambda i: (i, 0)),
            pl.BlockSpec((tm, 512), lambda i: (i, 0)),
            pl.BlockSpec((tm, 512), lambda i: (i, 0)),
            pl.BlockSpec((tm, 512), lambda i: (i, 0)),
            full((1, GDN_DV)), full((D_MODEL, D_MODEL)), full((1, D_MODEL)),
            full((D_MODEL, nhp * PEER_HALF)), full((nhp, PEER_NKEYS, PEER_HALF)),
        ],
        out_specs=[
            pl.BlockSpec((tm, D_MODEL), lambda i: (i, 0)),
            pl.BlockSpec((tm, D_MODEL), lambda i: (i, 0)),
            pl.BlockSpec((nhp, PEER_NKEYS, tm), lambda i: (0, 0, i)),
        ],
        out_shape=[
            jax.ShapeDtypeStruct((TOKENS, D_MODEL), F32),
            jax.ShapeDtypeStruct((TOKENS, D_MODEL), BF16),
            jax.ShapeDtypeStruct((nhp, PEER_NKEYS, TOKENS), F32),
        ],
        compiler_params=_cparams(("parallel",)),
        name="mix",
    )(x2, o_mla, o_f, o_b, z, gnw, w_out, ln2_w, peer_wq, sub_keys)


def _top16_lockstep(xs, idx_iota, n_sentinel, riota):
    n = len(xs)
    nrow = xs[0].shape[0]
    islices = [idx_iota[r0:r0 + 8] for r0 in range(0, nrow, 8)]

    def argmax(x):
        level = [(x[r0:r0 + 8], i8) for r0, i8 in zip(range(0, nrow, 8), islices)]
        while len(level) > 1:
            nxt = []
            for a in range(0, len(level) - 1, 2):
                (xl, il), (xr, ir) = level[a], level[a + 1]
                left = xl >= xr
                nxt.append((jnp.where(left, xl, xr), jnp.where(left, il, ir)))
            if len(level) % 2:
                nxt.append(level[-1])
            level = nxt
        v8, i8 = level[0]
        m = jnp.max(v8, axis=0, keepdims=True)
        return m, jnp.min(jnp.where(v8 == m, i8, n_sentinel), axis=0, keepdims=True)

    vals = [jnp.zeros((PEER_TOPK, LANES), F32)] * n
    idxs = [jnp.zeros((PEER_TOPK, LANES), F32)] * n
    for r in range(PEER_TOPK):
        best = [argmax(x) for x in xs]
        vals = [jnp.where(riota == r, m, v) for (m, _), v in zip(best, vals)]
        idxs = [jnp.where(riota == r, i, v) for (_, i), v in zip(best, idxs)]
        xs = [jnp.where(idx_iota == i, -jnp.inf, x) for x, (_, i) in zip(xs, best)]
    return vals, idxs


def _topk_pair(scores):
    kiota = lax.broadcasted_iota(jnp.int32, (PEER_NKEYS, LANES), 0).astype(F32)
    riota = lax.broadcasted_iota(jnp.int32, (PEER_TOPK, LANES), 0)
    riota_f = riota.astype(F32)
    r8 = lax.broadcasted_iota(jnp.int32, (8, LANES), 0).astype(F32)
    jlim = [PEER_TOPK // (i + 1) for i in range(PEER_TOPK)]
    cflat = jnp.concatenate([riota_f, PEER_TOPK + r8] + [PEER_TOPK * i + r8 for i in range(2, 8)]
                            + [PEER_TOPK * (8.0 + r8)], axis=0)
    ncand = float(PEER_TOPK * PEER_TOPK)
    vals, idxs = _top16_lockstep(scores, kiota, float(PEER_NKEYS), riota)
    cands = []
    for n in range(TOPK_HEADS_PER_ITER):
        s1, s2 = vals[2 * n], vals[2 * n + 1]
        s2lo = s2[0:8]
        parts = [s1[0:1] + s2, s1[1:2] + s2lo]
        parts += [jnp.where(r8 < jlim[i], s1[i:i + 1] + s2lo, -jnp.inf) for i in range(2, 8)]
        parts.append(s1[8:16] + s2[0:1])
        cands.append(jnp.concatenate(parts, axis=0))
    cvs, cfs = _top16_lockstep(cands, cflat, ncand, riota)
    out = []
    for n in range(TOPK_HEADS_PER_ITER):
        i1, i2 = idxs[2 * n], idxs[2 * n + 1]
        cv, cf = cvs[n], cfs[n]
        e = jnp.exp(cv - cv[0:1, :])
        gate = e / jnp.sum(e, axis=0, keepdims=True)
        fi = jnp.floor(cf * (1.0 / PEER_TOPK))
        fj = cf - fi * PEER_TOPK
        asel = jnp.zeros((PEER_TOPK, LANES), F32)
        bsel = jnp.zeros((PEER_TOPK, LANES), F32)
        for r in range(PEER_TOPK):
            av = jnp.sum(jnp.where(riota_f == fi[r:r + 1, :], i1, 0.0), axis=0, keepdims=True)
            bv = jnp.sum(jnp.where(riota_f == fj[r:r + 1, :], i2, 0.0), axis=0, keepdims=True)
            asel = jnp.where(riota == r, av, asel)
            bsel = jnp.where(riota == r, bv, bsel)
        out.append((asel, bsel, gate))
    return out


def _topk_kernel(st_ref, a_ref, b_ref, g_ref, stage_ref):
    def body(it, carry):
        first = it * (2 * TOPK_HEADS_PER_ITER)
        res = _topk_pair([st_ref[first + r] for r in range(2 * TOPK_HEADS_PER_ITER)])
        for n, sel in enumerate(res):
            rows = pl.ds(pl.multiple_of((it * TOPK_HEADS_PER_ITER + n) * PEER_TOPK, PEER_TOPK), PEER_TOPK)
            for c in range(3):
                stage_ref[c, rows, :] = sel[c]
        return carry

    lax.fori_loop(0, TOPK_PAIRS, body, 0)
    a_ref[...] = stage_ref[0].T
    b_ref[...] = stage_ref[1].T
    g_ref[...] = stage_ref[2].T


def _topk(scores_t, n_tokens):
    tt = TT_TOPK
    nhp = 2 * PEER_HEADS
    tok_spec = pl.BlockSpec((tt, PEER_SEL), lambda i: (i, 0))
    return pl.pallas_call(
        _topk_kernel,
        grid=(n_tokens // tt,),
        in_specs=[pl.BlockSpec((nhp, PEER_NKEYS, tt), lambda i: (0, 0, i))],
        out_specs=[tok_spec, tok_spec, tok_spec],
        out_shape=[jax.ShapeDtypeStruct((n_tokens, PEER_SEL), F32)] * 3,
        scratch_shapes=[pltpu.VMEM((3, PEER_SEL, tt), F32)],
        compiler_params=_cparams(("parallel",)),
        name="topk",
    )(scores_t)


def _peer_kernel(x_ref, u_ref, v_ref, a0_ref, b0_ref, g0_ref, st_ref, h_ref, fw_ref, o_ref,
                 gm_ref, sel_ref, stage_ref):
    i = pl.program_id(0)
    j = pl.program_id(1)
    tt = TT_PEER
    nblk = tt // LANES

    @pl.when((j == 0) & (i == 0))
    def _():
        sel_ref[0] = a0_ref[...]
        sel_ref[1] = b0_ref[...]
        sel_ref[2] = g0_ref[...]

    @pl.when((j == 0) & (i > 0))
    def _():
        for c in range(3):
            for blk in range(nblk):
                sel_ref[c, blk * LANES:(blk + 1) * LANES, :] = stage_ref[c, blk].T

    @pl.when(j == 0)
    def _():
        o_ref[...] = jnp.zeros(o_ref.shape, F32)
        row = lax.broadcasted_iota(jnp.int32, (PEER_NKEYS, PEER_SEL), 0)
        iota_k = row.astype(F32).astype(BF16)
        iota_a = jnp.where(row < A_HALF, 2 * row, 2 * (row - A_HALF) + 1).astype(F32).astype(BF16)
        zero = jnp.zeros((PEER_NKEYS, PEER_SEL), BF16)
        one = jnp.ones((PEER_NKEYS, PEER_SEL), BF16)

        def rows(vals, k):
            r16 = jnp.broadcast_to(vals[k:k + 1, :], (16, PEER_SEL)).astype(BF16)
            return jnp.concatenate([r16] * (PEER_NKEYS // 16), axis=0)

        def build(it, carry):
            base = pl.multiple_of(it * TOK_UNROLL, TOK_UNROLL)
            av = sel_ref[0, pl.ds(base, TOK_UNROLL), :]
            bv = sel_ref[1, pl.ds(base, TOK_UNROLL), :]
            gv = sel_ref[2, pl.ds(base, TOK_UNROLL), :] * 0.5
            for k in range(TOK_UNROLL):
                at = jnp.where(iota_a == rows(av, k), rows(gv, k), zero)
                bt = jnp.where(iota_k == rows(bv, k), one, zero)
                gt = _dot_nt(at, bt)
                row0 = pl.multiple_of((base + k) * G_PITCH, 8)
                gm_ref[pl.ds(row0, A_HALF), :] = pltpu.pack_elementwise(
                    [gt[:A_HALF], gt[A_HALF:]], packed_dtype=BF16)
            return carry

        lax.fori_loop(0, tt // TOK_UNROLL, build, 0)

    x = x_ref[...]
    inv_sqrt2 = jnp.concatenate([lax.rsqrt(_const_row(2.0))] * (TE_PEER // LANES), axis=1)
    nr = 2 * TOPK_HEADS_PER_ITER

    def sub_block(s, carry):
        unit = j * PEER_SUB + s
        pair = unit % TOPK_PAIRS
        blk = unit // TOPK_PAIRS
        for n, sel in enumerate(_topk_pair([st_ref[s * nr + r] for r in range(nr)])):
            rows16 = pl.ds(pl.multiple_of((pair * TOPK_HEADS_PER_ITER + n) * PEER_TOPK, PEER_TOPK), PEER_TOPK)
            for c in range(3):
                stage_ref[c, blk, rows16, :] = sel[c]

        es = pl.ds(pl.multiple_of(s * TE_PEER, TE_PEER), TE_PEER)
        hid = _dot_nt(x, u_ref[es, :])
        act = hid * (1.0 + lax.erf(hid * inv_sqrt2))
        r0 = unit * (A_PER // 2)
        gparts = []
        for rr in range(A_PER // 2):
            word = gm_ref[pl.ds(r0 + rr, tt, stride=G_PITCH), :]
            gparts.append(pltpu.unpack_elementwise(word, index=0, packed_dtype=BF16, unpacked_dtype=F32))
            gparts.append(pltpu.unpack_elementwise(word, index=1, packed_dtype=BF16, unpacked_dtype=F32))
        gsel = jnp.concatenate(gparts, axis=1)
        o_ref[...] += _dot((act * gsel).astype(BF16), v_ref[es, :])
        return carry

    lax.fori_loop(0, PEER_SUB, sub_block, 0)

    @pl.when(j == pl.num_programs(1) - 1)
    def _():
        y = h_ref[...] + o_ref[...]
        o_ref[...] = y * lax.rsqrt(jnp.mean(y * y, axis=-1, keepdims=True) + EPS) * fw_ref[...]


def _peer(n2, u_tab, v_tab, a_first, b_first, g_first, scores_t, h, fw):
    tt, te = TT_PEER, TE_PEER * PEER_SUB
    nt = TOKENS // tt
    nblk = tt // LANES
    assert PEER_EXPERTS // TE_PEER == TOPK_PAIRS * nblk and TOPK_PAIRS % PEER_SUB == 0
    tok = lambda w: pl.BlockSpec((tt, w), lambda i, j: (i, 0))
    first = pl.BlockSpec((tt, PEER_SEL), lambda i, j: (0, 0))
    steps_per_blk = TOPK_PAIRS // PEER_SUB
    nxt_scores = pl.BlockSpec(
        (2 * TOPK_HEADS_PER_ITER * PEER_SUB, PEER_NKEYS, LANES),
        lambda i, j: (j % steps_per_blk, 0, jnp.minimum(i + 1, nt - 1) * nblk + j // steps_per_blk))
    return pl.pallas_call(
        _peer_kernel,
        grid=(nt, PEER_EXPERTS // te),
        in_specs=[
            tok(D_MODEL),
            pl.BlockSpec((te, D_MODEL), lambda i, j: (j, 0)),
            pl.BlockSpec((te, D_MODEL), lambda i, j: (j, 0)),
            first, first, first,
            nxt_scores,
            tok(D_MODEL),
            pl.BlockSpec((1, D_MODEL), lambda i, j: (0, 0)),
        ],
        out_specs=tok(D_MODEL),
        out_shape=jax.ShapeDtypeStruct((TOKENS, D_MODEL), F32),
        scratch_shapes=[
            pltpu.VMEM((tt * G_PITCH, LANES), jnp.uint32),
            pltpu.VMEM((3, tt, PEER_SEL), F32),
            pltpu.VMEM((3, nblk, PEER_SEL, LANES), F32),
        ],
        compiler_params=_cparams(("arbitrary", "arbitrary")),
        name="peer",
    )(n2, u_tab, v_tab, a_first, b_first, g_first, scores_t, h, fw)


def kernel(x, positions, ln1_w, w_in, q_a_norm_w, w_q_up, kv_a_norm_w, w_kv_up, conv_w, a_log_f, dt_bias_f, a_log_b, dt_bias_b, gdn_norm_w, w_out, ln2_w, peer_wq, peer_sub_keys, peer_u, peer_v, final_norm_w):
    x2 = x.reshape(TOKENS, D_MODEL)

    w = w_in[0]
    mla_in = MLA_Q_RANK + MLA_KV_RANK + MLA_ROPE
    half = MLA_ROPE // 2
    kpe0 = MLA_Q_RANK + MLA_KV_RANK
    g0 = mla_in + GDN_QKV + GDN_W
    w_all = jnp.concatenate([
        w[:, :mla_in], w[:, kpe0 + half:mla_in], w[:, kpe0:kpe0 + half],
        w[:, mla_in:mla_in + GDN_QKV], w[:, mla_in + GDN_QKV:g0],
        w[:, g0:], jnp.zeros((D_MODEL, LANES - 4 * GDN_HEADS), F32)], axis=1).astype(BF16)
    mla, qkv, z, gates = _inproj(x2, ln1_w, w_all)

    wq = w_q_up[0].reshape(MLA_Q_RANK, MLA_HEADS, MLA_QK)
    zpad = jnp.zeros((MLA_Q_RANK, MLA_HEADS, LANES - MLA_ROPE), F32)
    wqn = wq[:, :, :MLA_NOPE].reshape(MLA_Q_RANK, -1).astype(BF16)
    rope = wq[:, :, MLA_NOPE:]
    wqr = jnp.concatenate([rope, zpad], axis=2).reshape(MLA_Q_RANK, -1).astype(BF16)
    rope_sw = jnp.concatenate([rope[:, :, half:], rope[:, :, :half]], axis=2)
    wqrs = jnp.concatenate([rope_sw, zpad], axis=2).reshape(MLA_Q_RANK, -1).astype(BF16)
    wkv = w_kv_up[0].reshape(MLA_KV_RANK, MLA_HEADS, MLA_NOPE + MLA_VDIM)
    wukt = jnp.transpose(wkv[:, :, :MLA_NOPE], (1, 2, 0)).astype(BF16)
    wuv = jnp.transpose(wkv[:, :, MLA_NOPE:], (1, 0, 2)).astype(BF16)
    inv = ROPE_THETA ** (-jnp.arange(0, MLA_ROPE, 2, dtype=F32) / MLA_ROPE)
    inv128 = jnp.tile(inv, 4).reshape(1, LANES)
    sgn128 = jnp.tile(jnp.concatenate([-jnp.ones((half,), F32), jnp.ones((half,), F32)]), 2).reshape(1, LANES)
    pos = positions.astype(F32).reshape(TOKENS, 1)
    q, k, v = _mla_prep(mla, pos, inv128, sgn128, q_a_norm_w, kv_a_norm_w, wqn, wqr, wqrs, wukt)
    o_mla = _attention(q, k, v, wuv)

    zeros4 = jnp.zeros((GDN_HEADS,), F32)
    lane_pad = jnp.zeros((LANES - 4 * GDN_HEADS,), F32)
    nalog = jnp.concatenate([-jnp.exp(a_log_f[0]), zeros4, -jnp.exp(a_log_b[0]), zeros4, lane_pad]).reshape(1, LANES)
    dtb = jnp.concatenate([dt_bias_f[0], zeros4, dt_bias_b[0], zeros4, lane_pad]).reshape(1, LANES)
    cw = jnp.concatenate([conv_w[0], jnp.zeros((8 - CONV_K, GDN_QKV), F32)], axis=0)
    gq, gk, gv, gc, gct = _gdn_prep(qkv, gates, cw, nalog, dtb)
    parts = _gdn_chunk(gq, gk, gv, gc, gct)
    o_f, o_b = _gdn_scan(parts[:5], parts[5:])

    sub_keys = peer_sub_keys[0].reshape(2 * PEER_HEADS, PEER_NKEYS, PEER_HALF).astype(BF16)
    h, n2, scores_t = _mix(x2, o_mla, o_f, o_b, z, gdn_norm_w, w_out[0].astype(BF16), ln2_w,
                           peer_wq[0].astype(BF16), sub_keys)
    a_first, b_first, g_first = _topk(scores_t[:, :, :TT_PEER], TT_PEER)
    out = _peer(n2, peer_u[0].astype(BF16), peer_v[0].astype(BF16), a_first, b_first, g_first, scores_t, h,
                final_norm_w.reshape(1, D_MODEL))
    return out.reshape(BATCH, SEQ, D_MODEL)
```

```python
import jax
import jax.numpy as jnp
from jax import lax
from jax.experimental import pallas as pl
from jax.experimental.pallas import tpu as pltpu

F32 = jnp.float32
BF16 = jnp.bfloat16

D_MODEL = 1024
BATCH = 2
SEQ = 8192
TOKENS = BATCH * SEQ

MLA_HEADS = 4
MLA_Q_RANK = 256
MLA_KV_RANK = 128
MLA_NOPE = 128
MLA_ROPE = 64
MLA_VDIM = 128
ROPE_THETA = 10000.0
MLA_QK = MLA_NOPE + MLA_ROPE
MLA_QPAD = 256
MLA_W = MLA_HEADS * MLA_VDIM
MLA_KPE = MLA_Q_RANK + MLA_KV_RANK
MLA_BLK = MLA_KPE + 2 * MLA_ROPE

GDN_HEADS = 4
GDN_DK = 128
GDN_DV = 128
CONV_K = 5
CHUNK = 64
GDN_W = GDN_HEADS * GDN_DK
GDN_QKV = 3 * GDN_W
N_CHUNKS = SEQ // CHUNK
OFF_QKV = MLA_BLK
OFF_Z = OFF_QKV + GDN_QKV
OFF_GATES = OFF_Z + GDN_W

PEER_HEADS = 8
PEER_NKEYS = 128
PEER_EXPERTS = PEER_NKEYS * PEER_NKEYS
PEER_TOPK = 16
PEER_HALF = 128
PEER_SEL = PEER_HEADS * PEER_TOPK
EPS = 1e-6

LANES = 128
VMEM_LIMIT = 56 * 1024 * 1024

TM_INPROJ = 1024
TM_MLA = 1024
TQ_ATTN = 1024
TK_ATTN = 2048
TM_GDN = 512
R_CHUNK = 256
U_ROWS = 128
SCAN_CH = 8
TM_MIX = 512
TT_TOPK = 128
TOPK_HEADS_PER_ITER = 2
TOPK_PAIRS = PEER_HEADS // TOPK_HEADS_PER_ITER
TT_PEER = 512
TE_PEER = 1024
PEER_SUB = 2
A_PER = TE_PEER // PEER_NKEYS
A_HALF = PEER_NKEYS // 2
G_PITCH = A_HALF + 8
TOK_UNROLL = 64

NEG_BIG = -1e30


def _const_row(value):
    return jnp.full((1, LANES), value, F32)


def _cparams(sem, fuse_inputs=None):
    return pltpu.CompilerParams(dimension_semantics=sem, vmem_limit_bytes=VMEM_LIMIT,
                                allow_input_fusion=fuse_inputs)


def _dot(a, b):
    return jnp.dot(a, b, preferred_element_type=F32)


def _dot_nt(a, b):
    return lax.dot_general(a, b, (((1,), (1,)), ((), ())), preferred_element_type=F32)


def _inproj_kernel(x_ref, lnw_ref, w_ref, mla_ref, qkv_ref, z_ref, g_ref):
    x = x_ref[...]
    y = x * lax.rsqrt(jnp.mean(x * x, axis=-1, keepdims=True) + EPS) * lnw_ref[...]
    p = _dot(y.astype(BF16), w_ref[...])
    mla_ref[...] = p[:, :OFF_QKV]
    qkv_ref[...] = p[:, OFF_QKV:OFF_Z]
    z_ref[...] = p[:, OFF_Z:OFF_GATES]
    g_ref[...] = p[:, OFF_GATES:OFF_GATES + LANES]


def _inproj(x2, ln1_w, w_all):
    tm = TM_INPROJ
    n = w_all.shape[1]
    return pl.pallas_call(
        _inproj_kernel,
        grid=(TOKENS // tm,),
        in_specs=[
            pl.BlockSpec((tm, D_MODEL), lambda i: (i, 0)),
            pl.BlockSpec((1, D_MODEL), lambda i: (0, 0)),
            pl.BlockSpec((D_MODEL, n), lambda i: (0, 0)),
        ],
        out_specs=[
            pl.BlockSpec((tm, MLA_BLK), lambda i: (i, 0)),
            pl.BlockSpec((tm, GDN_QKV), lambda i: (i, 0)),
            pl.BlockSpec((tm, GDN_W), lambda i: (i, 0)),
            pl.BlockSpec((tm, LANES), lambda i: (i, 0)),
        ],
        out_shape=[
            jax.ShapeDtypeStruct((TOKENS, MLA_BLK), F32),
            jax.ShapeDtypeStruct((TOKENS, GDN_QKV), F32),
            jax.ShapeDtypeStruct((TOKENS, GDN_W), F32),
            jax.ShapeDtypeStruct((TOKENS, LANES), F32),
        ],
        compiler_params=_cparams(("parallel",)),
        name="inproj",
    )(x2, ln1_w, w_all)


def _mla_prep_kernel(mla_ref, pos_ref, inv_ref, sgn_ref, qnw_ref, kvnw_ref,
                     wqn_ref, wqr_ref, wqrs_ref, wukt_ref, q_ref, k_ref, v_ref):
    m = mla_ref[...]
    cq = m[:, :MLA_Q_RANK]
    ckv = m[:, MLA_Q_RANK:MLA_Q_RANK + MLA_KV_RANK]
    kp = m[:, MLA_KPE:MLA_BLK]
    cqn = (cq * lax.rsqrt(jnp.mean(cq * cq, axis=-1, keepdims=True) + EPS) * qnw_ref[...]).astype(BF16)
    ckvn = ckv * lax.rsqrt(jnp.mean(ckv * ckv, axis=-1, keepdims=True) + EPS) * kvnw_ref[...]
    ang = pos_ref[...] * inv_ref[...]
    cos = jnp.cos(ang)
    sin = jnp.sin(ang) * sgn_ref[...]
    cos4 = jnp.concatenate([cos] * MLA_HEADS, axis=1)
    sin4 = jnp.concatenate([sin] * MLA_HEADS, axis=1)
    qn = _dot(cqn, wqn_ref[...])
    qr = _dot(cqn, wqr_ref[...]) * cos4 + _dot(cqn, wqrs_ref[...]) * sin4
    scale = lax.rsqrt(_const_row(float(MLA_QK))) / jnp.log(_const_row(2.0))
    scale = jnp.concatenate([scale, scale], axis=1)
    for h in range(MLA_HEADS):
        qa = _dot(qn[:, h * 128:(h + 1) * 128].astype(BF16), wukt_ref[h])
        q_ref[h] = (jnp.concatenate([qa, qr[:, h * 128:(h + 1) * 128]], axis=1) * scale).astype(BF16)
    lane = lax.broadcasted_iota(jnp.int32, kp.shape, 1)
    kr = kp * cos + pltpu.roll(kp, 64, axis=1) * sin
    kr = jnp.where(lane < MLA_ROPE, kr, 0.0)
    k_ref[...] = jnp.concatenate([ckvn, kr], axis=1).astype(BF16)
    ones_col = jnp.where(lane == 0, 1.0, 0.0)
    v_ref[...] = jnp.concatenate([ckvn, ones_col], axis=1).astype(BF16)


def _mla_prep(mla, pos, inv128, sgn128, qnw, kvnw, wqn, wqr, wqrs, wukt):
    tm = TM_MLA
    full = lambda shape: pl.BlockSpec(shape, lambda i: (0,) * len(shape))
    return pl.pallas_call(
        _mla_prep_kernel,
        grid=(TOKENS // tm,),
        in_specs=[
            pl.BlockSpec((tm, MLA_BLK), lambda i: (i, 0)),
            pl.BlockSpec((tm, 1), lambda i: (i, 0)),
            full((1, LANES)), full((1, LANES)), full((1, MLA_Q_RANK)), full((1, MLA_KV_RANK)),
            full((MLA_Q_RANK, MLA_HEADS * LANES)), full((MLA_Q_RANK, MLA_HEADS * LANES)),
            full((MLA_Q_RANK, MLA_HEADS * LANES)),
            full((MLA_HEADS, MLA_NOPE, MLA_KV_RANK)),
        ],
        out_specs=[
            pl.BlockSpec((MLA_HEADS, tm, MLA_QPAD), lambda i: (0, i, 0)),
            pl.BlockSpec((tm, MLA_QPAD), lambda i: (i, 0)),
            pl.BlockSpec((tm, MLA_QPAD), lambda i: (i, 0)),
        ],
        out_shape=[
            jax.ShapeDtypeStruct((MLA_HEADS, TOKENS, MLA_QPAD), BF16),
            jax.ShapeDtypeStruct((TOKENS, MLA_QPAD), BF16),
            jax.ShapeDtypeStruct((TOKENS, MLA_QPAD), BF16),
        ],
        compiler_params=_cparams(("parallel",)),
        name="mla_prep",
    )(mla, pos, inv128, sgn128, qnw, kvnw, wqn, wqr, wqrs, wukt)


def _attn_kernel(q_ref, k_ref, v_ref, wuv_ref, o_ref, m_ref, acc_ref):
    j = pl.program_id(2)
    tq = TQ_ATTN

    @pl.when(j == 0)
    def _():
        m_ref[...] = jnp.full(m_ref.shape, NEG_BIG, F32)
        acc_ref[...] = jnp.zeros(acc_ref.shape, F32)

    k = k_ref[...]
    v = v_ref[...]
    hsl = [slice(h * tq, (h + 1) * tq) for h in range(MLA_HEADS)]
    ss = [_dot_nt(q_ref[h], k) for h in range(MLA_HEADS)]
    m_prevs = [m_ref[hs, :] for hs in hsl]
    m_news = [jnp.maximum(mp, jnp.max(s, axis=1, keepdims=True)) for mp, s in zip(m_prevs, ss)]
    alphas = [jnp.exp2(mp - mn) for mp, mn in zip(m_prevs, m_news)]
    ps = [jnp.exp2((s - jnp.concatenate([mn] * (TK_ATTN // LANES), axis=1)).astype(BF16)) for s, mn in zip(ss, m_news)]
    pvs = [_dot(p, v) for p in ps]
    for hs, al, pv, mn in zip(hsl, alphas, pvs, m_news):
        acc_ref[hs, :] = jnp.concatenate([al] * (MLA_QPAD // LANES), axis=1) * acc_ref[hs, :] + pv
        m_ref[hs, :] = mn

    @pl.when(j == pl.num_programs(2) - 1)
    def _():
        outs = []
        for h, hs in enumerate(hsl):
            o_lat = acc_ref[hs, :MLA_KV_RANK] / acc_ref[hs, MLA_KV_RANK:MLA_KV_RANK + 1]
            outs.append(_dot(o_lat.astype(BF16), wuv_ref[h]))
        o_ref[...] = jnp.concatenate(outs, axis=1)


def _attention(q, k, v, wuv):
    tq, tk = TQ_ATTN, TK_ATTN
    nq, nk = SEQ // tq, SEQ // tk
    return pl.pallas_call(
        _attn_kernel,
        grid=(BATCH, nq, nk),
        in_specs=[
            pl.BlockSpec((MLA_HEADS, tq, MLA_QPAD), lambda b, i, j: (0, b * nq + i, 0)),
            pl.BlockSpec((tk, MLA_QPAD), lambda b, i, j: (b * nk + j, 0)),
            pl.BlockSpec((tk, MLA_QPAD), lambda b, i, j: (b * nk + j, 0)),
            pl.BlockSpec((MLA_HEADS, 128, 128), lambda b, i, j: (0, 0, 0)),
        ],
        out_specs=pl.BlockSpec((tq, MLA_HEADS * MLA_VDIM), lambda b, i, j: (b * nq + i, 0)),
        out_shape=jax.ShapeDtypeStruct((TOKENS, MLA_HEADS * MLA_VDIM), F32),
        scratch_shapes=[
            pltpu.VMEM((MLA_HEADS * tq, LANES), F32),
            pltpu.VMEM((MLA_HEADS * tq, MLA_QPAD), F32),
        ],
        compiler_params=_cparams(("parallel", "parallel", "arbitrary")),
        name="attn",
    )(q, k, v, wuv)


def _softplus(x):
    return jnp.maximum(x, 0.0) + jnp.log1p(jnp.exp(-jnp.abs(x)))


def _gdn_prep_kernel(x_ref, prev_ref, next_ref, g_ref, cw_ref, nalog_ref, dtb_ref,
                     q_ref, k_ref, v_ref, gc_ref, gct_ref):
    i = pl.program_id(0)
    tm = TM_GDN
    tiles_per_seq = SEQ // tm
    first = (i % tiles_per_seq) == 0
    last = (i % tiles_per_seq) == tiles_per_seq - 1
    prev = jnp.where(first, 0.0, prev_ref[...])
    nxt = jnp.where(last, 0.0, next_ref[...])
    xe = jnp.concatenate([prev, x_ref[...], nxt], axis=0)
    pad = CONV_K // 2
    y = None
    for t in range(CONV_K):
        off = 8 - pad + t
        term = xe[off:off + tm, :] * cw_ref[t:t + 1, :]
        y = term if y is None else y + term
    y = y * jax.nn.sigmoid(y)

    def l2n(a):
        return a * lax.rsqrt(jnp.sum(a * a, axis=-1, keepdims=True) + EPS)

    qs = lax.rsqrt(_const_row(float(GDN_DK)))
    for h in range(GDN_HEADS):
        hs = slice(h * GDN_DK, (h + 1) * GDN_DK)
        q_ref[:, hs] = l2n(y[:, h * GDN_DK:(h + 1) * GDN_DK]) * qs
        k_ref[:, hs] = l2n(y[:, GDN_W + h * GDN_DK:GDN_W + (h + 1) * GDN_DK])
    v_ref[...] = y[:, 2 * GDN_W:]

    g = g_ref[...]
    gdec = nalog_ref[...] * _softplus(g + dtb_ref[...])
    beta = jax.nn.sigmoid(g)
    row = lax.broadcasted_iota(jnp.int32, g.shape, 0)
    lane = lax.broadcasted_iota(jnp.int32, g.shape, 1)
    pos = row & (CHUNK - 1)
    pre = gdec
    suf = gdec
    s = 1
    while s < CHUNK:
        pre = pre + jnp.where(pos >= s, pltpu.roll(pre, s, axis=0), 0.0)
        suf = suf + jnp.where(pos < CHUNK - s, pltpu.roll(suf, tm - s, axis=0), 0.0)
        s *= 2
    out = jnp.where(lane < 4, pre, jnp.where((lane >= 8) & (lane < 12), suf, beta))
    gc_ref[...] = out
    gct_ref[...] = out.T[:16, :]


def _gdn_prep(qkv, gates, conv_w, nalog, dtb):
    tm = TM_GDN
    nb8 = TOKENS // 8
    return pl.pallas_call(
        _gdn_prep_kernel,
        grid=(TOKENS // tm,),
        in_specs=[
            pl.BlockSpec((tm, GDN_QKV), lambda i: (i, 0)),
            pl.BlockSpec((8, GDN_QKV), lambda i: (jnp.maximum(i * (tm // 8) - 1, 0), 0)),
            pl.BlockSpec((8, GDN_QKV), lambda i: (jnp.minimum((i + 1) * (tm // 8), nb8 - 1), 0)),
            pl.BlockSpec((tm, LANES), lambda i: (i, 0)),
            pl.BlockSpec((8, GDN_QKV), lambda i: (0, 0)),
            pl.BlockSpec((1, LANES), lambda i: (0, 0)),
            pl.BlockSpec((1, LANES), lambda i: (0, 0)),
        ],
        out_specs=[
            pl.BlockSpec((tm, GDN_W), lambda i: (i, 0)),
            pl.BlockSpec((tm, GDN_W), lambda i: (i, 0)),
            pl.BlockSpec((tm, GDN_W), lambda i: (i, 0)),
            pl.BlockSpec((tm, LANES), lambda i: (i, 0)),
            pl.BlockSpec((16, tm), lambda i: (0, i)),
        ],
        out_shape=[
            jax.ShapeDtypeStruct((TOKENS, GDN_W), F32),
            jax.ShapeDtypeStruct((TOKENS, GDN_W), F32),
            jax.ShapeDtypeStruct((TOKENS, GDN_W), F32),
            jax.ShapeDtypeStruct((TOKENS, LANES), F32),
            jax.ShapeDtypeStruct((16, TOKENS), F32),
        ],
        compiler_params=_cparams(("parallel",)),
        name="gdn_prep",
    )(qkv, qkv, qkv, gates, conv_w, nalog, dtb)


def _gdn_chunk_kernel(q_ref, k_ref, v_ref, gc_ref, gct_ref,
                      oi_f, qe_f, kc_f, nc_f, eg_f, oi_b, qe_b, kc_b, nc_b, eg_b):
    U = U_ROWS
    ncu = U // CHUNK
    rowi = lax.broadcasted_iota(jnp.int32, (U, U), 0)
    colj = lax.broadcasted_iota(jnp.int32, (U, U), 1)
    same64 = (rowi >> 6) == (colj >> 6)
    same32 = (rowi >> 5) == (colj >> 5)
    same16 = (rowi >> 4) == (colj >> 4)
    eye = (rowi == colj).astype(F32)
    tri = (rowi >= colj, rowi <= colj)
    stri = (rowi > colj, rowi < colj)
    rchunk = lax.broadcasted_iota(jnp.int32, (U, GDN_DK), 0) >> 6
    gates = gc_ref[...]
    gates_t = gct_ref[...]
    outs = ((oi_f, qe_f, kc_f, nc_f, eg_f), (oi_b, qe_b, kc_b, nc_b, eg_b))

    def bdot(a, b):
        return _dot(a.astype(BF16), b.astype(BF16))

    chains = []
    for u in range(R_CHUNK // U):
        rs = slice(u * U, (u + 1) * U)
        for h in range(GDN_HEADS):
            hs = slice(h * GDN_DK, (h + 1) * GDN_DK)
            q = q_ref[rs, hs]
            k = k_ref[rs, hs]
            v = v_ref[rs, hs]
            kb16 = k.astype(BF16)
            kk = _dot_nt(kb16, kb16)
            qk = _dot_nt(q.astype(BF16), kb16)
            for d in range(2):
                c0 = 8 * d
                gcc = gates[rs, c0 + h:c0 + h + 1]
                bcol = gates[rs, c0 + 4 + h:c0 + 5 + h]
                gcr = gates_t[c0 + h:c0 + h + 1, rs]
                incl = same64 & tri[d]
                dec = jnp.where(incl, jnp.exp(jnp.where(incl, gcc - gcr, 0.0)), 0.0)
                lmat = jnp.where(stri[d], bcol * kk * dec, 0.0)
                chains.append(dict(u=u, h=h, d=d, rs=rs, hs=hs, q=q, k=k, v=v, gcc=gcc, bcol=bcol,
                                   lmat=lmat, attn=jnp.where(incl, qk * dec, 0.0)))

    l0s = [jnp.where(same16, c["lmat"], 0.0) for c in chains]
    xs = [eye - l0 for l0 in l0s]
    ps = [bdot(l0, l0) for l0 in l0s]
    xs = [x + bdot(x, p) for x, p in zip(xs, ps)]
    ps = [bdot(p, p) for p in ps]
    xs = [x + bdot(x, p) for x, p in zip(xs, ps)]
    ps = [bdot(p, p) for p in ps]
    xs = [x + bdot(x, p) for x, p in zip(xs, ps)]
    off32 = same32 & jnp.logical_not(same16)
    ts = [bdot(jnp.where(off32, c["lmat"], 0.0), x) for c, x in zip(chains, xs)]
    xs = [x - bdot(x, t) for x, t in zip(xs, ts)]
    off64 = jnp.logical_not(same32)
    ts = [bdot(jnp.where(off64, c["lmat"], 0.0), x) for c, x in zip(chains, xs)]
    xs = [x - bdot(x, t) for x, t in zip(xs, ts)]

    egs = [jnp.exp(c["gcc"]) for c in chains]
    rhs = [jnp.concatenate([c["v"] * c["bcol"], c["k"] * (c["bcol"] * eg)], axis=1) for c, eg in zip(chains, egs)]
    sols = [bdot(x, r) for x, r in zip(xs, rhs)]
    ows = [bdot(c["attn"], s) for c, s in zip(chains, sols)]
    cats, kdts = [], []
    for c, eg, sol, ow in zip(chains, egs, sols, ows):
        oi_ref, qe_ref = outs[c["d"]][:2]
        oi_ref[c["rs"], c["hs"]] = ow[:, :GDN_DV]
        qe_ref[c["rs"], c["hs"]] = (c["q"] * eg - ow[:, GDN_DV:]).astype(BF16)
        ends = [ch * CHUNK + (CHUNK - 1 if c["d"] == 0 else 0) for ch in range(ncu)]
        glc = jnp.concatenate([jnp.broadcast_to(c["gcc"][r:r + 1, :], (CHUNK, 1)) for r in ends], axis=0)
        kdts.append((c["k"] * jnp.exp(glc - c["gcc"])).T)
        u_part, w_part = sol[:, :GDN_DV], sol[:, GDN_DV:]
        cats.append(jnp.concatenate([jnp.where(rchunk == ch, w_part, 0.0) for ch in range(ncu)]
                                    + [jnp.where(rchunk == ch, u_part, 0.0) for ch in range(ncu)], axis=1))
        c["ends"] = ends
    kns = [bdot(kdt, cat) for kdt, cat in zip(kdts, cats)]
    for c, kn in zip(chains, kns):
        kc_ref, nc_ref, eg_ref = outs[c["d"]][2:]
        for ch in range(ncu):
            ci = c["u"] * ncu + ch
            h = c["h"]
            kc_ref[ci, h] = kn[:, ch * 128:(ch + 1) * 128].astype(BF16)
            nc_ref[ci, h] = kn[:, (ncu + ch) * 128:(ncu + ch + 1) * 128].astype(BF16)
            r = c["ends"][ch]
            eg_ref[ci, h:h + 1, :] = jnp.broadcast_to(jnp.exp(c["gcc"][r:r + 1, :]), (1, LANES))


def _gdn_chunk(q, k, v, gc, gct):
    R = R_CHUNK
    ncs = R // CHUNK
    nct = TOKENS // CHUNK
    row_spec = pl.BlockSpec((R, GDN_W), lambda i: (i, 0))
    out_specs, out_shape = [], []
    for _ in range(2):
        out_specs += [
            pl.BlockSpec((R, GDN_W), lambda i: (i, 0)),
            pl.BlockSpec((R, GDN_W), lambda i: (i, 0)),
            pl.BlockSpec((ncs, GDN_HEADS, 128, 128), lambda i: (i, 0, 0, 0)),
            pl.BlockSpec((ncs, GDN_HEADS, 128, 128), lambda i: (i, 0, 0, 0)),
            pl.BlockSpec((ncs, GDN_HEADS, LANES), lambda i: (i, 0, 0)),
        ]
        out_shape += [
            jax.ShapeDtypeStruct((TOKENS, GDN_W), F32),
            jax.ShapeDtypeStruct((TOKENS, GDN_W), BF16),
            jax.ShapeDtypeStruct((nct, GDN_HEADS, 128, 128), BF16),
            jax.ShapeDtypeStruct((nct, GDN_HEADS, 128, 128), BF16),
            jax.ShapeDtypeStruct((nct, GDN_HEADS, LANES), F32),
        ]
    return pl.pallas_call(
        _gdn_chunk_kernel,
        grid=(TOKENS // R,),
        in_specs=[row_spec, row_spec, row_spec,
                  pl.BlockSpec((R, LANES), lambda i: (i, 0)),
                  pl.BlockSpec((16, R), lambda i: (0, i))],
        out_specs=out_specs,
        out_shape=out_shape,
        compiler_params=_cparams(("parallel",)),
        name="gdn_chunk",
    )(q, k, v, gc, gct)


def _gdn_scan_kernel(oi_f, qe_f, kc_f, nc_f, eg_f, oi_b, qe_b, kc_b, nc_b, eg_b,
                     of_ref, ob_ref, s_ref):
    c = pl.program_id(1)

    @pl.when(c == 0)
    def _():
        s_ref[...] = jnp.zeros(s_ref.shape, F32)

    dirs = ((oi_f, qe_f, kc_f, nc_f, eg_f, of_ref), (oi_b, qe_b, kc_b, nc_b, eg_b, ob_ref))
    chains = [(d, h) for d in range(2) for h in range(GDN_HEADS)]
    states = [s_ref[d * GDN_HEADS + h] for d, h in chains]
    for t in range(SCAN_CH):
        nxt = []
        for (d, h), s in zip(chains, states):
            oi, qe, kc, nc, eg, o_ref = dirs[d]
            lc = t if d == 0 else SCAN_CH - 1 - t
            rows = slice(lc * CHUNK, (lc + 1) * CHUNK)
            hs = slice(h * GDN_DK, (h + 1) * GDN_DK)
            sb = s.astype(BF16)
            o_ref[rows, hs] = oi[rows, hs] + _dot(qe[rows, hs], sb)
            nxt.append(eg[lc, h:h + 1, :] * s - _dot(kc[lc, h], sb) + nc[lc, h].astype(F32))
        states = nxt
    for (d, h), s in zip(chains, states):
        s_ref[d * GDN_HEADS + h] = s


def _gdn_scan(fwd, bwd):
    nc = N_CHUNKS // SCAN_CH
    rows = SCAN_CH * CHUNK

    def specs(idx):
        return [
            pl.BlockSpec((rows, GDN_W), lambda b, c: (idx(b, c), 0)),
            pl.BlockSpec((rows, GDN_W), lambda b, c: (idx(b, c), 0)),
            pl.BlockSpec((SCAN_CH, GDN_HEADS, 128, 128), lambda b, c: (idx(b, c), 0, 0, 0)),
            pl.BlockSpec((SCAN_CH, GDN_HEADS, 128, 128), lambda b, c: (idx(b, c), 0, 0, 0)),
            pl.BlockSpec((SCAN_CH, GDN_HEADS, LANES), lambda b, c: (idx(b, c), 0, 0)),
        ]

    fidx = lambda b, c: b * nc + c
    bidx = lambda b, c: b * nc + (nc - 1 - c)
    return pl.pallas_call(
        _gdn_scan_kernel,
        grid=(BATCH, nc),
        in_specs=specs(fidx) + specs(bidx),
        out_specs=[
            pl.BlockSpec((rows, GDN_W), lambda b, c: (fidx(b, c), 0)),
            pl.BlockSpec((rows, GDN_W), lambda b, c: (bidx(b, c), 0)),
        ],
        out_shape=[jax.ShapeDtypeStruct((TOKENS, GDN_W), F32)] * 2,
        scratch_shapes=[pltpu.VMEM((2 * GDN_HEADS, GDN_DK, GDN_DV), F32)],
        compiler_params=_cparams(("parallel", "arbitrary")),
        name="gdn_scan",
    )(*fwd, *bwd)


def _mix_kernel(x_ref, om_ref, of_ref, ob_ref, z_ref, gnw_ref, wo_ref, ln2_ref, wq_ref, sk_ref,
                h_ref, n2_ref, st_ref):
    o = of_ref[...] + ob_ref[...]
    z = z_ref[...]
    gparts = []
    for hh in range(GDN_HEADS):
        oh = o[:, hh * GDN_DV:(hh + 1) * GDN_DV]
        oh = oh * lax.rsqrt(jnp.mean(oh * oh, axis=-1, keepdims=True) + EPS) * gnw_ref[...]
        gparts.append(oh)
    og = jnp.concatenate(gparts, axis=1) * (z * jax.nn.sigmoid(z))
    mixed = _dot(om_ref[...].astype(BF16), wo_ref[:MLA_W, :]) + _dot(og.astype(BF16), wo_ref[MLA_W:, :])
    hres = x_ref[...] + mixed
    h_ref[...] = hres
    n2 = hres * lax.rsqrt(jnp.mean(hres * hres, axis=-1, keepdims=True) + EPS) * ln2_ref[...]
    n2b = n2.astype(BF16)
    n2_ref[...] = n2b
    pq = _dot(n2b, wq_ref[...]).astype(BF16)
    for hp in range(2 * PEER_HEADS):
        st_ref[hp] = _dot_nt(sk_ref[hp], pq[:, hp * PEER_HALF:(hp + 1) * PEER_HALF])


def _mix(x2, o_mla, o_f, o_b, z, gnw, w_out, ln2_w, peer_wq, sub_keys):
    tm = TM_MIX
    nhp = 2 * PEER_HEADS
    full = lambda shape: pl.BlockSpec(shape, lambda i: (0,) * len(shape))
    return pl.pallas_call(
        _mix_kernel,
        grid=(TOKENS // tm,),
        in_specs=[
            pl.BlockSpec((tm, D_MODEL), lambda i: (i, 0)),
            pl.BlockSpec((tm, MLA_W), lambda i: (i, 0)),
            pl.BlockSpec((tm, GDN_W), lambda i: (i, 0)),
            pl.BlockSpec((tm, GDN_W), lambda i: (i, 0)),
            pl.BlockSpec((tm, GDN_W), lambda i: (i, 0)),
            full((1, GDN_DV)), full((D_MODEL, D_MODEL)), full((1, D_MODEL)),
            full((D_MODEL, nhp * PEER_HALF)), full((nhp, PEER_NKEYS, PEER_HALF)),
        ],
        out_specs=[
            pl.BlockSpec((tm, D_MODEL), lambda i: (i, 0)),
            pl.BlockSpec((tm, D_MODEL), lambda i: (i, 0)),
            pl.BlockSpec((nhp, PEER_NKEYS, tm), lambda i: (0, 0, i)),
        ],
        out_shape=[
            jax.ShapeDtypeStruct((TOKENS, D_MODEL), F32),
            jax.ShapeDtypeStruct((TOKENS, D_MODEL), BF16),
            jax.ShapeDtypeStruct((nhp, PEER_NKEYS, TOKENS), F32),
        ],
        compiler_params=_cparams(("parallel",)),
        name="mix",
    )(x2, o_mla, o_f, o_b, z, gnw, w_out, ln2_w, peer_wq, sub_keys)


def _top16_lockstep(xs, idx_iota, n_sentinel, riota):
    n = len(xs)
    nrow = xs[0].shape[0]
    islices = [idx_iota[r0:r0 + 8] for r0 in range(0, nrow, 8)]

    def argmax(x):
        level = [(x[r0:r0 + 8], i8) for r0, i8 in zip(range(0, nrow, 8), islices)]
        while len(level) > 1:
            nxt = []
            for a in range(0, len(level) - 1, 2):
                (xl, il), (xr, ir) = level[a], level[a + 1]
                left = xl >= xr
                nxt.append((jnp.where(left, xl, xr), jnp.where(left, il, ir)))
            if len(level) % 2:
                nxt.append(level[-1])
            level = nxt
        v8, i8 = level[0]
        m = jnp.max(v8, axis=0, keepdims=True)
        return m, jnp.min(jnp.where(v8 == m, i8, n_sentinel), axis=0, keepdims=True)

    vals = [jnp.zeros((PEER_TOPK, LANES), F32)] * n
    idxs = [jnp.zeros((PEER_TOPK, LANES), F32)] * n
    for r in range(PEER_TOPK):
        best = [argmax(x) for x in xs]
        vals = [jnp.where(riota == r, m, v) for (m, _), v in zip(best, vals)]
        idxs = [jnp.where(riota == r, i, v) for (_, i), v in zip(best, idxs)]
        xs = [jnp.where(idx_iota == i, -jnp.inf, x) for x, (_, i) in zip(xs, best)]
    return vals, idxs


def _topk_pair(scores):
    kiota = lax.broadcasted_iota(jnp.int32, (PEER_NKEYS, LANES), 0).astype(F32)
    riota = lax.broadcasted_iota(jnp.int32, (PEER_TOPK, LANES), 0)
    riota_f = riota.astype(F32)
    r8 = lax.broadcasted_iota(jnp.int32, (8, LANES), 0).astype(F32)
    jlim = [PEER_TOPK // (i + 1) for i in range(PEER_TOPK)]
    cflat = jnp.concatenate([riota_f, PEER_TOPK + r8] + [PEER_TOPK * i + r8 for i in range(2, 8)]
                            + [PEER_TOPK * (8.0 + r8)], axis=0)
    ncand = float(PEER_TOPK * PEER_TOPK)
    vals, idxs = _top16_lockstep(scores, kiota, float(PEER_NKEYS), riota)
    cands = []
    for n in range(TOPK_HEADS_PER_ITER):
        s1, s2 = vals[2 * n], vals[2 * n + 1]
        s2lo = s2[0:8]
        parts = [s1[0:1] + s2, s1[1:2] + s2lo]
        parts += [jnp.where(r8 < jlim[i], s1[i:i + 1] + s2lo, -jnp.inf) for i in range(2, 8)]
        parts.append(s1[8:16] + s2[0:1])
        cands.append(jnp.concatenate(parts, axis=0))
    cvs, cfs = _top16_lockstep(cands, cflat, ncand, riota)
    out = []
    for n in range(TOPK_HEADS_PER_ITER):
        i1, i2 = idxs[2 * n], idxs[2 * n + 1]
        cv, cf = cvs[n], cfs[n]
        e = jnp.exp(cv - cv[0:1, :])
        gate = e / jnp.sum(e, axis=0, keepdims=True)
        fi = jnp.floor(cf * (1.0 / PEER_TOPK))
        fj = cf - fi * PEER_TOPK
        asel = jnp.zeros((PEER_TOPK, LANES), F32)
        bsel = jnp.zeros((PEER_TOPK, LANES), F32)
        for r in range(PEER_TOPK):
            av = jnp.sum(jnp.where(riota_f == fi[r:r + 1, :], i1, 0.0), axis=0, keepdims=True)
            bv = jnp.sum(jnp.where(riota_f == fj[r:r + 1, :], i2, 0.0), axis=0, keepdims=True)
            asel = jnp.where(riota == r, av, asel)
            bsel = jnp.where(riota == r, bv, bsel)
        out.append((asel, bsel, gate))
    return out


def _topk_kernel(st_ref, a_ref, b_ref, g_ref, stage_ref):
    def body(it, carry):
        first = it * (2 * TOPK_HEADS_PER_ITER)
        res = _topk_pair([st_ref[first + r] for r in range(2 * TOPK_HEADS_PER_ITER)])
        for n, sel in enumerate(res):
            rows = pl.ds(pl.multiple_of((it * TOPK_HEADS_PER_ITER + n) * PEER_TOPK, PEER_TOPK), PEER_TOPK)
            for c in range(3):
                stage_ref[c, rows, :] = sel[c]
        return carry

    lax.fori_loop(0, TOPK_PAIRS, body, 0)
    a_ref[...] = stage_ref[0].T
    b_ref[...] = stage_ref[1].T
    g_ref[...] = stage_ref[2].T


def _topk(scores_t, n_tokens):
    tt = TT_TOPK
    nhp = 2 * PEER_HEADS
    tok_spec = pl.BlockSpec((tt, PEER_SEL), lambda i: (i, 0))
    return pl.pallas_call(
        _topk_kernel,
        grid=(n_tokens // tt,),
        in_specs=[pl.BlockSpec((nhp, PEER_NKEYS, tt), lambda i: (0, 0, i))],
        out_specs=[tok_spec, tok_spec, tok_spec],
        out_shape=[jax.ShapeDtypeStruct((n_tokens, PEER_SEL), F32)] * 3,
        scratch_shapes=[pltpu.VMEM((3, PEER_SEL, tt), F32)],
        compiler_params=_cparams(("parallel",)),
        name="topk",
    )(scores_t)


def _peer_kernel(x_ref, u_ref, v_ref, a0_ref, b0_ref, g0_ref, st_ref, h_ref, fw_ref, o_ref,
                 gm_ref, sel_ref, stage_ref):
    i = pl.program_id(0)
    j = pl.program_id(1)
    tt = TT_PEER
    nblk = tt // LANES

    @pl.when((j == 0) & (i == 0))
    def _():
        sel_ref[0] = a0_ref[...]
        sel_ref[1] = b0_ref[...]
        sel_ref[2] = g0_ref[...]

    @pl.when((j == 0) & (i > 0))
    def _():
        for c in range(3):
            for blk in range(nblk):
                sel_ref[c, blk * LANES:(blk + 1) * LANES, :] = stage_ref[c, blk].T

    @pl.when(j == 0)
    def _():
        o_ref[...] = jnp.zeros(o_ref.shape, F32)
        row = lax.broadcasted_iota(jnp.int32, (PEER_NKEYS, PEER_SEL), 0)
        iota_k = row.astype(F32).astype(BF16)
        iota_a = jnp.where(row < A_HALF, 2 * row, 2 * (row - A_HALF) + 1).astype(F32).astype(BF16)
        zero = jnp.zeros((PEER_NKEYS, PEER_SEL), BF16)
        one = jnp.ones((PEER_NKEYS, PEER_SEL), BF16)

        def rows(vals, k):
            r16 = jnp.broadcast_to(vals[k:k + 1, :], (16, PEER_SEL)).astype(BF16)
            return jnp.concatenate([r16] * (PEER_NKEYS // 16), axis=0)

        def build(it, carry):
            base = pl.multiple_of(it * TOK_UNROLL, TOK_UNROLL)
            av = sel_ref[0, pl.ds(base, TOK_UNROLL), :]
            bv = sel_ref[1, pl.ds(base, TOK_UNROLL), :]
            gv = sel_ref[2, pl.ds(base, TOK_UNROLL), :] * 0.5
            for k in range(TOK_UNROLL):
                at = jnp.where(iota_a == rows(av, k), rows(gv, k), zero)
                bt = jnp.where(iota_k == rows(bv, k), one, zero)
                gt = _dot_nt(at, bt)
                row0 = pl.multiple_of((base + k) * G_PITCH, 8)
                gm_ref[pl.ds(row0, A_HALF), :] = pltpu.pack_elementwise(
                    [gt[:A_HALF], gt[A_HALF:]], packed_dtype=BF16)
            return carry

        lax.fori_loop(0, tt // TOK_UNROLL, build, 0)

    x = x_ref[...]
    inv_sqrt2 = jnp.concatenate([lax.rsqrt(_const_row(2.0))] * (TE_PEER // LANES), axis=1)
    nr = 2 * TOPK_HEADS_PER_ITER
    for s in range(PEER_SUB):
        unit = j * PEER_SUB + s
        pair = unit % TOPK_PAIRS
        blk = unit // TOPK_PAIRS
        for n, sel in enumerate(_topk_pair([st_ref[s * nr + r] for r in range(nr)])):
            rows16 = pl.ds(pl.multiple_of((pair * TOPK_HEADS_PER_ITER + n) * PEER_TOPK, PEER_TOPK), PEER_TOPK)
            for c in range(3):
                stage_ref[c, blk, rows16, :] = sel[c]

        es = slice(s * TE_PEER, (s + 1) * TE_PEER)
        hid = _dot_nt(x, u_ref[es, :])
        act = hid * (1.0 + lax.erf(hid * inv_sqrt2))
        r0 = unit * (A_PER // 2)
        gparts = []
        for rr in range(A_PER // 2):
            word = gm_ref[pl.ds(r0 + rr, tt, stride=G_PITCH), :]
            gparts.append(pltpu.unpack_elementwise(word, index=0, packed_dtype=BF16, unpacked_dtype=F32))
            gparts.append(pltpu.unpack_elementwise(word, index=1, packed_dtype=BF16, unpacked_dtype=F32))
        gsel = jnp.concatenate(gparts, axis=1)
        o_ref[...] += _dot((act * gsel).astype(BF16), v_ref[es, :])

    @pl.when(j == pl.num_programs(1) - 1)
    def _():
        y = h_ref[...] + o_ref[...]
        o_ref[...] = y * lax.rsqrt(jnp.mean(y * y, axis=-1, keepdims=True) + EPS) * fw_ref[...]


def _peer(n2, u_tab, v_tab, a_first, b_first, g_first, scores_t, h, fw):
    tt, te = TT_PEER, TE_PEER * PEER_SUB
    nt = TOKENS // tt
    nblk = tt // LANES
    assert PEER_EXPERTS // TE_PEER == TOPK_PAIRS * nblk and TOPK_PAIRS % PEER_SUB == 0
    tok = lambda w: pl.BlockSpec((tt, w), lambda i, j: (i, 0))
    first = pl.BlockSpec((tt, PEER_SEL), lambda i, j: (0, 0))
    steps_per_blk = TOPK_PAIRS // PEER_SUB
    nxt_scores = pl.BlockSpec(
        (2 * TOPK_HEADS_PER_ITER * PEER_SUB, PEER_NKEYS, LANES),
        lambda i, j: (j % steps_per_blk, 0, jnp.minimum(i + 1, nt - 1) * nblk + j // steps_per_blk))
    return pl.pallas_call(
        _peer_kernel,
        grid=(nt, PEER_EXPERTS // te),
        in_specs=[
            tok(D_MODEL),
            pl.BlockSpec((te, D_MODEL), lambda i, j: (j, 0)),
            pl.BlockSpec((te, D_MODEL), lambda i, j: (j, 0)),
            first, first, first,
            nxt_scores,
            tok(D_MODEL),
            pl.BlockSpec((1, D_MODEL), lambda i, j: (0, 0)),
        ],
        out_specs=tok(D_MODEL),
        out_shape=jax.ShapeDtypeStruct((TOKENS, D_MODEL), F32),
        scratch_shapes=[
            pltpu.VMEM((tt * G_PITCH, LANES), jnp.uint32),
            pltpu.VMEM((3, tt, PEER_SEL), F32),
            pltpu.VMEM((3, nblk, PEER_SEL, LANES), F32),
        ],
        compiler_params=_cparams(("arbitrary", "arbitrary"),
                                 fuse_inputs=[False, True, True, False, False, False, False, False, False]),
        name="peer",
    )(n2, u_tab, v_tab, a_first, b_first, g_first, scores_t, h, fw)


def kernel(x, positions, ln1_w, w_in, q_a_norm_w, w_q_up, kv_a_norm_w, w_kv_up, conv_w, a_log_f, dt_bias_f, a_log_b, dt_bias_b, gdn_norm_w, w_out, ln2_w, peer_wq, peer_sub_keys, peer_u, peer_v, final_norm_w):
    x2 = x.reshape(TOKENS, D_MODEL)

    w = w_in[0]
    mla_in = MLA_Q_RANK + MLA_KV_RANK + MLA_ROPE
    half = MLA_ROPE // 2
    kpe0 = MLA_Q_RANK + MLA_KV_RANK
    g0 = mla_in + GDN_QKV + GDN_W
    w_all = jnp.concatenate([
        w[:, :mla_in], w[:, kpe0 + half:mla_in], w[:, kpe0:kpe0 + half],
        w[:, mla_in:mla_in + GDN_QKV], w[:, mla_in + GDN_QKV:g0],
        w[:, g0:], jnp.zeros((D_MODEL, LANES - 4 * GDN_HEADS), F32)], axis=1).astype(BF16)
    mla, qkv, z, gates = _inproj(x2, ln1_w, w_all)

    wq = w_q_up[0].reshape(MLA_Q_RANK, MLA_HEADS, MLA_QK)
    zpad = jnp.zeros((MLA_Q_RANK, MLA_HEADS, LANES - MLA_ROPE), F32)
    wqn = wq[:, :, :MLA_NOPE].reshape(MLA_Q_RANK, -1).astype(BF16)
    rope = wq[:, :, MLA_NOPE:]
    wqr = jnp.concatenate([rope, zpad], axis=2).reshape(MLA_Q_RANK, -1).astype(BF16)
    rope_sw = jnp.concatenate([rope[:, :, half:], rope[:, :, :half]], axis=2)
    wqrs = jnp.concatenate([rope_sw, zpad], axis=2).reshape(MLA_Q_RANK, -1).astype(BF16)
    wkv = w_kv_up[0].reshape(MLA_KV_RANK, MLA_HEADS, MLA_NOPE + MLA_VDIM)
    wukt = jnp.transpose(wkv[:, :, :MLA_NOPE], (1, 2, 0)).astype(BF16)
    wuv = jnp.transpose(wkv[:, :, MLA_NOPE:], (1, 0, 2)).astype(BF16)
    inv = ROPE_THETA ** (-jnp.arange(0, MLA_ROPE, 2, dtype=F32) / MLA_ROPE)
    inv128 = jnp.tile(inv, 4).reshape(1, LANES)
    sgn128 = jnp.tile(jnp.concatenate([-jnp.ones((half,), F32), jnp.ones((half,), F32)]), 2).reshape(1, LANES)
    pos = positions.astype(F32).reshape(TOKENS, 1)
    q, k, v = _mla_prep(mla, pos, inv128, sgn128, q_a_norm_w, kv_a_norm_w, wqn, wqr, wqrs, wukt)
    o_mla = _attention(q, k, v, wuv)

    zeros4 = jnp.zeros((GDN_HEADS,), F32)
    lane_pad = jnp.zeros((LANES - 4 * GDN_HEADS,), F32)
    nalog = jnp.concatenate([-jnp.exp(a_log_f[0]), zeros4, -jnp.exp(a_log_b[0]), zeros4, lane_pad]).reshape(1, LANES)
    dtb = jnp.concatenate([dt_bias_f[0], zeros4, dt_bias_b[0], zeros4, lane_pad]).reshape(1, LANES)
    cw = jnp.concatenate([conv_w[0], jnp.zeros((8 - CONV_K, GDN_QKV), F32)], axis=0)
    gq, gk, gv, gc, gct = _gdn_prep(qkv, gates, cw, nalog, dtb)
    parts = _gdn_chunk(gq, gk, gv, gc, gct)
    o_f, o_b = _gdn_scan(parts[:5], parts[5:])

    sub_keys = peer_sub_keys[0].reshape(2 * PEER_HEADS, PEER_NKEYS, PEER_HALF).astype(BF16)
    h, n2, scores_t = _mix(x2, o_mla, o_f, o_b, z, gdn_norm_w, w_out[0].astype(BF16), ln2_w,
                           peer_wq[0].astype(BF16), sub_keys)
    a_first, b_first, g_first = _topk(scores_t[:, :, :TT_PEER], TT_PEER)
    out = _peer(n2, peer_u[0].astype(BF16), peer_v[0].astype(BF16), a_first, b_first, g_first, scores_t, h,
                final_norm_w.reshape(1, D_MODEL))
    return out.reshape(BATCH, SEQ, D_MODEL)
```
